```python
import math
import jax, jax.numpy as jnp
from jax import lax
import numpy as np

D_MODEL = 2048
BATCH = 2
SEQ = 8192
DEPTH = 1

CHUNK = 64
CONV_DIM = 1024
CONV_WIDTH = 31
N_HEADS = 8
HEAD_DIM = 64
ATTN_DIM = N_HEADS * 2 * HEAD_DIM
ROT_DIM = HEAD_DIM // 4
ROPE_THETA = 500000.0
Q_BLOCK = 128
N_BRANCH = 2
IN_COLS = 2 * CONV_DIM + 3 * ATTN_DIM + N_BRANCH * D_MODEL
N_EXPERTS = 32
TOP_K = 4
D_FF = 2048
SWIGLU_LIMIT = 7.0
SWIGLU_ALPHA = 1.702
MOE_BLOCK = 512
NORM_EPS = 1e-6
LN_EPS = 1e-5

kernel_name = "hybrid_conv_diffattn_moe_block"


def rms_norm(x, g):
    xf = x.astype(jnp.float32)
    y = xf * lax.rsqrt(jnp.mean(xf * xf, axis=-1, keepdims=True) + NORM_EPS)
    return (y * g.astype(jnp.float32)).astype(x.dtype)


def layer_norm(x, g, b):
    xf = x.astype(jnp.float32)
    mu = jnp.mean(xf, axis=-1, keepdims=True)
    var = jnp.mean(jnp.square(xf - mu), axis=-1, keepdims=True)
    y = (xf - mu) * lax.rsqrt(var + LN_EPS) * g.astype(jnp.float32) + b.astype(jnp.float32)
    return y.astype(x.dtype)


def rope_tables(positions):
    inv = ROPE_THETA ** (-jnp.arange(0, ROT_DIM, 2, dtype=jnp.float32) / ROT_DIM)
    ang = positions.astype(jnp.float32)[..., None] * inv
    return jnp.cos(ang), jnp.sin(ang)


def partial_rope(t, cos, sin):
    tf = t.astype(jnp.float32)
    x1 = tf[..., : ROT_DIM // 2]
    x2 = tf[..., ROT_DIM // 2: ROT_DIM]
    c = cos[:, :, None, None, :]
    s = sin[:, :, None, None, :]
    rot = jnp.concatenate([x1 * c - x2 * s, x2 * c + x1 * s], axis=-1)
    return jnp.concatenate([rot.astype(t.dtype), t[..., ROT_DIM:]], axis=-1)


def conformer_conv(a, dw_w, dw_b, ln_g, ln_b):
    val, gate = jnp.split(a, 2, axis=-1)
    h = val * jax.nn.sigmoid(gate)
    h = lax.conv_general_dilated(
        h, dw_w[:, None, :].astype(h.dtype), window_strides=(1,),
        padding=[(CONV_WIDTH - 1, 0)],
        dimension_numbers=("NWC", "WIO", "NWC"),
        feature_group_count=CONV_DIM) + dw_b
    h = layer_norm(h, ln_g, ln_b)
    return jax.nn.silu(h)


def diff_attention(q, k, v, cos, sin, lq1, lk1, lq2, lk2, subln_g, lam_init):
    B, S = q.shape[0], q.shape[1]
    q = partial_rope(q.reshape(B, S, N_HEADS, 2, HEAD_DIM), cos, sin)
    k = partial_rope(k.reshape(B, S, N_HEADS, 2, HEAD_DIM), cos, sin)
    v = v.reshape(B, S, N_HEADS, 2 * HEAD_DIM)
    f32 = jnp.float32
    lam = (jnp.exp(jnp.sum(lq1.astype(f32) * lk1.astype(f32)))
           - jnp.exp(jnp.sum(lq2.astype(f32) * lk2.astype(f32))) + lam_init)
    scale = HEAD_DIM ** -0.5
    n_blk = S // Q_BLOCK
    q_blocks = jnp.moveaxis(q.reshape(B, n_blk, Q_BLOCK, N_HEADS, 2, HEAD_DIM), 1, 0)
    key_chunk = jnp.arange(S) // CHUNK

    def one_block(args):
        qb, i = args
        q_chunk = (i * Q_BLOCK + jnp.arange(Q_BLOCK)) // CHUNK
        mask = key_chunk[None, :] <= q_chunk[:, None]
        s = jnp.einsum("bqhcd,bkhcd->bhcqk", qb, k).astype(f32) * scale
        p = jax.nn.softmax(jnp.where(mask, s, -jnp.inf), axis=-1)
        attn = p[:, :, 0] - lam * p[:, :, 1]
        return jnp.einsum("bhqk,bkhe->bqhe", attn.astype(v.dtype), v)

    o = lax.map(one_block, (q_blocks, jnp.arange(n_blk)))
    o = jnp.moveaxis(o, 0, 1).reshape(B, S, N_HEADS, 2 * HEAD_DIM)
    o = rms_norm(o, subln_g) * (1.0 - lam_init)
    return o.reshape(B, S, ATTN_DIM)


def moe_ffn(u, router_w, router_b, w_gu, b_gu, w_down, b_down):
    T, D = u.shape
    logits = u.astype(jnp.float32) @ router_w.astype(jnp.float32) + router_b.astype(jnp.float32)
    top_val, top_idx = lax.top_k(logits, TOP_K)
    weights = jax.nn.softmax(top_val, axis=-1)
    A = T * TOP_K
    flat_e = top_idx.reshape(A)
    flat_tok = jnp.arange(A, dtype=jnp.int32) // TOP_K
    order = jnp.argsort(flat_e)
    sorted_e = flat_e[order]
    sorted_tok = flat_tok[order]
    sorted_w = weights.reshape(A)[order]
    counts = jnp.bincount(flat_e, length=N_EXPERTS)
    padded = (counts + MOE_BLOCK - 1) // MOE_BLOCK * MOE_BLOCK
    pad_end = jnp.cumsum(padded)
    pad_start = pad_end - padded
    start = jnp.cumsum(counts) - counts
    dest = pad_start[sorted_e] + jnp.arange(A) - start[sorted_e]
    n_blocks = -(-A // MOE_BLOCK) + N_EXPERTS
    rows_tok = jnp.zeros((n_blocks * MOE_BLOCK,), jnp.int32).at[dest].set(sorted_tok)
    block_e = jnp.minimum(
        jnp.searchsorted(pad_end, jnp.arange(n_blocks) * MOE_BLOCK, side="right"), N_EXPERTS - 1)
    xs = u[rows_tok].reshape(n_blocks, MOE_BLOCK, D)

    def expert_block(args):
        xb, e = args
        gu = xb @ w_gu[e] + b_gu[e]
        gate = jnp.minimum(gu[..., ::2], SWIGLU_LIMIT)
        up = jnp.clip(gu[..., 1::2], -SWIGLU_LIMIT, SWIGLU_LIMIT)
        act = (up + 1.0) * gate * jax.nn.sigmoid(SWIGLU_ALPHA * gate)
        return act @ w_down[e] + b_down[e]

    ys = lax.map(expert_block, (xs, block_e)).reshape(n_blocks * MOE_BLOCK, D)
    y_assign = ys[dest] * sorted_w[:, None].astype(ys.dtype)
    return jax.ops.segment_sum(y_assign, sorted_tok, num_segments=T)


def setup_inputs(seed: int = 0) -> dict:
    key = jax.random.key(seed)
    ks = jax.random.split(key, 26)
    f32 = jnp.float32
    L, D = DEPTH, D_MODEL

    def nrm(k, shape, scale):
        return jax.random.normal(k, shape, f32) * scale

    offsets = jax.random.randint(ks[1], (BATCH, 1), 0, 4096, dtype=jnp.int32)
    positions = offsets + jnp.arange(SEQ, dtype=jnp.int32)[None, :]
    return {
        "x": nrm(ks[0], (BATCH, SEQ, D), 1.0),
        "positions": positions,
        "norm_mix_g": 1.0 + nrm(ks[2], (L, D), 0.02),
        "w_in": nrm(ks[3], (L, D, IN_COLS), D ** -0.5),
        "gate_b": nrm(ks[4], (L, N_BRANCH * D), 0.01),
        "conv_dw_w": nrm(ks[5], (L, CONV_WIDTH, CONV_DIM), CONV_WIDTH ** -0.5),
        "conv_dw_b": nrm(ks[6], (L, CONV_DIM), 0.01),
        "conv_ln_g": 1.0 + nrm(ks[7], (L, CONV_DIM), 0.02),
        "conv_ln_b": nrm(ks[8], (L, CONV_DIM), 0.01),
        "w_conv_out": nrm(ks[9], (L, CONV_DIM, D), CONV_DIM ** -0.5),
        "lambda_q1": nrm(ks[10], (L, HEAD_DIM), 0.1),
        "lambda_k1": nrm(ks[11], (L, HEAD_DIM), 0.1),
        "lambda_q2": nrm(ks[12], (L, HEAD_DIM), 0.1),
        "lambda_k2": nrm(ks[13], (L, HEAD_DIM), 0.1),
        "attn_subln_g": 1.0 + nrm(ks[14], (L, 2 * HEAD_DIM), 0.02),
        "w_attn_out": nrm(ks[15], (L, ATTN_DIM, D), ATTN_DIM ** -0.5),
        "w_out": nrm(ks[16], (L, D, D), D ** -0.5),
        "norm_ffn_g": 1.0 + nrm(ks[17], (L, D), 0.02),
        "router_w": nrm(ks[18], (L, D, N_EXPERTS), D ** -0.5),
        "router_b": nrm(ks[19], (L, N_EXPERTS), 0.01),
        "w_gu": nrm(ks[20], (L, N_EXPERTS, D, 2 * D_FF), D ** -0.5),
        "b_gu": nrm(ks[21], (L, N_EXPERTS, 2 * D_FF), 0.01),
        "w_down": nrm(ks[22], (L, N_EXPERTS, D_FF, D), D_FF ** -0.5),
        "b_down": nrm(ks[23], (L, N_EXPERTS, D), 0.01),
        "norm_final_g": 1.0 + nrm(ks[24], (D,), 0.02),
    }


def reference(x, positions, norm_mix_g, w_in, gate_b, conv_dw_w, conv_dw_b, conv_ln_g, conv_ln_b,
              w_conv_out, lambda_q1, lambda_k1, lambda_q2, lambda_k2, attn_subln_g, w_attn_out,
              w_out, norm_ffn_g, router_w, router_b, w_gu, b_gu, w_down, b_down, norm_final_g):
    B, S, D = x.shape
    cos, sin = rope_tables(positions)
    splits = [2 * CONV_DIM, 2 * CONV_DIM + ATTN_DIM, 2 * CONV_DIM + 2 * ATTN_DIM,
              2 * CONV_DIM + 3 * ATTN_DIM]
    h = x
    for l in range(DEPTH):
        lam_init = 0.8 - 0.6 * math.exp(-0.3 * l)
        u = rms_norm(h, norm_mix_g[l])
        proj = u @ w_in[l]
        conv_in, q, k, v, gate_logits = jnp.split(proj, splits, axis=-1)
        y_conv = conformer_conv(conv_in, conv_dw_w[l], conv_dw_b[l],
                                conv_ln_g[l], conv_ln_b[l]) @ w_conv_out[l]
        y_attn = diff_attention(q, k, v, cos, sin, lambda_q1[l], lambda_k1[l], lambda_q2[l],
                                lambda_k2[l], attn_subln_g[l], lam_init) @ w_attn_out[l]
        g = jax.nn.sigmoid(gate_logits + gate_b[l]).reshape(B, S, N_BRANCH, D)
        merged = g[:, :, 0] * y_conv + g[:, :, 1] * y_attn
        h = h + merged @ w_out[l]
        uf = rms_norm(h, norm_ffn_g[l]).reshape(B * S, D)
        h = h + moe_ffn(uf, router_w[l], router_b[l], w_gu[l], b_gu[l],
                        w_down[l], b_down[l]).reshape(B, S, D)
    return rms_norm(h, norm_final_g)
```

```python
import functools

import jax
import jax.numpy as jnp
from jax import lax
from jax.experimental import pallas as pl
from jax.experimental.pallas import tpu as pltpu

F32 = jnp.float32
BF16 = jnp.bfloat16

D_MODEL = 2048
CHUNK = 64
CONV_DIM = 1024
CONV_WIDTH = 31
N_HEADS = 8
HEAD_DIM = 64
ATTN_DIM = N_HEADS * 2 * HEAD_DIM
ROT_DIM = HEAD_DIM // 4
ROPE_THETA = 500000.0
N_BRANCH = 2
IN_COLS = 2 * CONV_DIM + 3 * ATTN_DIM + N_BRANCH * D_MODEL
N_EXPERTS = 32
TOP_K = 4
D_FF = 2048
SWIGLU_LIMIT = 7.0
SWIGLU_ALPHA = 1.702
NORM_EPS = 1e-6
LN_EPS = 1e-5
LAM_INIT = 0.2

LANES = 128
VMEM_LIMIT = 56 * 1024 * 1024

COL_Q = 2 * CONV_DIM
COL_K = COL_Q + ATTN_DIM
COL_V = COL_K + ATTN_DIM
COL_G = COL_V + ATTN_DIM

TM_IN = 1024
TN_IN = 1024
TS_CONV = 256
HALO = 32
TQ = 512
TK = 512
TM_MIX = 256
TR = 512
TB = 256
M_BLK = 512
TF = 1024
ROW_TILES = D_MODEL // LANES


def _cparams(sem):
    return pltpu.CompilerParams(dimension_semantics=sem, vmem_limit_bytes=VMEM_LIMIT)


def _inproj_kernel(x_ref, g_ref, w_ref, c_ref, s1_ref, s2_ref, o_ref, u_ref):
    j = pl.program_id(1)

    @pl.when(j == 0)
    def _():
        x = x_ref[...]
        ms = jnp.mean(x * x, axis=-1, keepdims=True)
        u_ref[...] = (x * lax.rsqrt(ms + NORM_EPS) * g_ref[...]).astype(BF16)

    acc = jnp.dot(u_ref[...], w_ref[...], preferred_element_type=F32)
    q_tile = COL_Q // TN_IN
    k_tile = COL_K // TN_IN
    is_rope = jnp.logical_or(j == q_tile, j == k_tile)

    @pl.when(is_rope)
    def _():
        scale = jnp.where(j == q_tile, HEAD_DIM ** -0.5, 1.0).astype(F32)
        c = c_ref[...]
        s1 = s1_ref[...]
        s2 = s2_ref[...]
        for hh in range(TN_IN // LANES):
            t = acc[:, hh * LANES:(hh + 1) * LANES]
            r = (t * c + pltpu.roll(t, LANES - ROT_DIM // 2, 1) * s1
                 + pltpu.roll(t, ROT_DIM // 2, 1) * s2)
            o_ref[:, hh * LANES:(hh + 1) * LANES] = (r * scale).astype(BF16)

    @pl.when(jnp.logical_not(is_rope))
    def _():
        o_ref[...] = acc.astype(BF16)


def _inproj(x2, g, w_bf, ctab, s1tab, s2tab):
    T = x2.shape[0]
    return pl.pallas_call(
        _inproj_kernel,
        grid=(T // TM_IN, IN_COLS // TN_IN),
        in_specs=[
            pl.BlockSpec((TM_IN, D_MODEL), lambda i, j: (i, 0)),
            pl.BlockSpec((1, D_MODEL), lambda i, j: (0, 0)),
            pl.BlockSpec((D_MODEL, TN_IN), lambda i, j: (0, j)),
            pl.BlockSpec((TM_IN, LANES), lambda i, j: (i, 0)),
            pl.BlockSpec((TM_IN, LANES), lambda i, j: (i, 0)),
            pl.BlockSpec((TM_IN, LANES), lambda i, j: (i, 0)),
        ],
        out_specs=pl.BlockSpec((TM_IN, TN_IN), lambda i, j: (i, j)),
        out_shape=jax.ShapeDtypeStruct((T, IN_COLS), BF16),
        scratch_shapes=[pltpu.VMEM((TM_IN, D_MODEL), BF16)],
        compiler_params=_cparams(("parallel", "arbitrary")),
    )(x2, g, w_bf, ctab, s1tab, s2tab)


def _conv_kernel(val_ref, gate_ref, hval_ref, hgate_ref, dww_ref, dwb_ref, lng_ref, lnb_ref,
                 wco_ref, ga0_ref, ga1_ref, gab_ref, o_ref, hbuf, cbuf):
    i = pl.program_id(1)
    ts = TS_CONV
    halo = hval_ref[...].astype(F32) * jax.nn.sigmoid(hgate_ref[...].astype(F32))
    hbuf[0:HALO, :] = jnp.where(i == 0, 0.0, halo)
    hbuf[HALO:HALO + ts, :] = val_ref[...].astype(F32) * jax.nn.sigmoid(gate_ref[...].astype(F32))

    base = HALO - (CONV_WIDTH - 1)

    def lane_chunk(c, carry):
        l0 = pl.multiple_of(c * LANES, LANES)
        acc = jnp.zeros((ts, LANES), F32) + dwb_ref[:, pl.ds(l0, LANES)]
        for j in range(CONV_WIDTH):
            acc = acc + dww_ref[j:j + 1, pl.ds(l0, LANES)] * hbuf[base + j:base + j + ts, pl.ds(l0, LANES)]
        cbuf[:, pl.ds(l0, LANES)] = acc
        return carry

    lax.fori_loop(0, CONV_DIM // LANES, lane_chunk, 0)

    y = cbuf[...]
    mu = jnp.mean(y, axis=-1, keepdims=True)
    yc = y - mu
    var = jnp.mean(yc * yc, axis=-1, keepdims=True)
    yn = yc * lax.rsqrt(var + LN_EPS) * lng_ref[...] + lnb_ref[...]
    a = yn * jax.nn.sigmoid(yn)
    out = jnp.dot(a.astype(BF16), wco_ref[...], preferred_element_type=F32)
    half = D_MODEL // 2
    g0 = jax.nn.sigmoid(ga0_ref[...].astype(F32) + gab_ref[:, 0:half])
    g1 = jax.nn.sigmoid(ga1_ref[...].astype(F32) + gab_ref[:, half:D_MODEL])
    o_ref[:, 0:half] = (out[:, 0:half] * g0).astype(BF16)
    o_ref[:, half:D_MODEL] = (out[:, half:D_MODEL] * g1).astype(BF16)


def _conv_branch(proj, B, S, dww, dwb, lng, lnb, wco_bf, gate_b_a):
    T = B * S
    nts = S // TS_CONV
    hpt = TS_CONV // HALO
    gcol = COL_G // CONV_DIM

    def halo_map(col):
        return lambda b, i: (jnp.maximum((b * nts + i) * hpt - 1, 0), col)

    const = lambda b, i: (0, 0)
    return pl.pallas_call(
        _conv_kernel,
        grid=(B, nts),
        in_specs=[
            pl.BlockSpec((TS_CONV, CONV_DIM), lambda b, i: (b * nts + i, 0)),
            pl.BlockSpec((TS_CONV, CONV_DIM), lambda b, i: (b * nts + i, 1)),
            pl.BlockSpec((HALO, CONV_DIM), halo_map(0)),
            pl.BlockSpec((HALO, CONV_DIM), halo_map(1)),
            pl.BlockSpec((CONV_WIDTH, CONV_DIM), const),
            pl.BlockSpec((1, CONV_DIM), const),
            pl.BlockSpec((1, CONV_DIM), const),
            pl.BlockSpec((1, CONV_DIM), const),
            pl.BlockSpec((CONV_DIM, D_MODEL), const),
            pl.BlockSpec((TS_CONV, CONV_DIM), lambda b, i: (b * nts + i, gcol)),
            pl.BlockSpec((TS_CONV, CONV_DIM), lambda b, i: (b * nts + i, gcol + 1)),
            pl.BlockSpec((1, D_MODEL), const),
        ],
        out_specs=pl.BlockSpec((TS_CONV, D_MODEL), lambda b, i: (b * nts + i, 0)),
        out_shape=jax.ShapeDtypeStruct((T, D_MODEL), BF16),
        scratch_shapes=[pltpu.VMEM((HALO + TS_CONV, CONV_DIM), F32),
                        pltpu.VMEM((TS_CONV, CONV_DIM), F32)],
        compiler_params=_cparams(("parallel", "arbitrary")),
    )(proj, proj, proj, proj, dww, dwb, lng, lnb, wco_bf, proj, proj, gate_b_a)


def _attn_kernel(q_ref, k_ref, v_ref, lam_ref, sg_ref, o_ref, qs_ref, m_ref, l_ref, acc_ref):
    qi = pl.program_id(2)
    q = q_ref[...]
    lane = lax.broadcasted_iota(jnp.int32, (TQ, 2 * HEAD_DIM), 1)
    zero = jnp.zeros_like(q)
    qs_ref[0:TQ, :] = jnp.where(lane < HEAD_DIM, q, zero)
    qs_ref[TQ:2 * TQ, :] = jnp.where(lane >= HEAD_DIM, q, zero)
    m_ref[...] = jnp.full(m_ref.shape, -jnp.inf, F32)
    l_ref[...] = jnp.zeros(l_ref.shape, F32)
    acc_ref[...] = jnp.zeros(acc_ref.shape, F32)

    def step(kc, masked):
        k0 = pl.multiple_of(kc * TK, TK)
        kb = k_ref[pl.ds(k0, TK), :]
        vb = v_ref[pl.ds(k0, TK), :]
        s = lax.dot_general(qs_ref[...], kb, (((1,), (1,)), ((), ())),
                            preferred_element_type=F32)
        if masked:
            row = lax.broadcasted_iota(jnp.int32, (2 * TQ, TK), 0)
            col = lax.broadcasted_iota(jnp.int32, (2 * TQ, TK), 1)
            q_chunk = jnp.bitwise_and(row, TQ - 1) // CHUNK
            s = jnp.where(col // CHUNK <= q_chunk, s, -jnp.inf)
        m_old = m_ref[...]
        m_new = jnp.maximum(m_old, jnp.max(s, axis=-1, keepdims=True))
        alpha = jnp.exp(m_old - m_new)
        p = jnp.exp(s - m_new)
        l_ref[...] = alpha * l_ref[...] + jnp.sum(p, axis=-1, keepdims=True)
        acc_ref[...] = alpha * acc_ref[...] + jnp.dot(p.astype(BF16), vb, preferred_element_type=F32)
        m_ref[...] = m_new

    def body(kc, carry):
        step(kc, False)
        return carry

    lax.fori_loop(0, qi, body, 0)
    step(qi, True)

    lam_v = lam_ref[...]
    lam = (jnp.exp(jnp.sum(lam_v[0:1, :] * lam_v[1:2, :], axis=-1, keepdims=True))
           - jnp.exp(jnp.sum(lam_v[2:3, :] * lam_v[3:4, :], axis=-1, keepdims=True)) + LAM_INIT)
    o0 = acc_ref[0:TQ, :] / l_ref[0:TQ, :]
    o1 = acc_ref[TQ:2 * TQ, :] / l_ref[TQ:2 * TQ, :]
    o = o0 - lam * o1
    ms = jnp.mean(o * o, axis=-1, keepdims=True)
    o = o * lax.rsqrt(ms + NORM_EPS) * sg_ref[...] * (1.0 - LAM_INIT)
    o_ref[...] = o.astype(BF16)


def _attention(proj, B, S, lam_vecs, subln_g):
    T = B * S
    nq = S // TQ
    hw = 2 * HEAD_DIM
    qc, kc, vc = COL_Q // hw, COL_K // hw, COL_V // hw
    return pl.pallas_call(
        _attn_kernel,
        grid=(B, N_HEADS, nq),
        in_specs=[
            pl.BlockSpec((TQ, hw), lambda b, h, i: (b * nq + i, qc + h)),
            pl.BlockSpec((S, hw), lambda b, h, i: (b, kc + h)),
            pl.BlockSpec((S, hw), lambda b, h, i: (b, vc + h)),
            pl.BlockSpec((4, HEAD_DIM), lambda b, h, i: (0, 0)),
            pl.BlockSpec((1, hw), lambda b, h, i: (0, 0)),
        ],
        out_specs=pl.BlockSpec((TQ, hw), lambda b, h, i: (b * nq + i, h)),
        out_shape=jax.ShapeDtypeStruct((T, ATTN_DIM), BF16),
        scratch_shapes=[pltpu.VMEM((2 * TQ, hw), BF16),
                        pltpu.VMEM((2 * TQ, 1), F32),
                        pltpu.VMEM((2 * TQ, 1), F32),
                        pltpu.VMEM((2 * TQ, hw), F32)],
        compiler_params=_cparams(("parallel", "parallel", "arbitrary")),
    )(proj, proj, proj, lam_vecs, subln_g)


def _mix_kernel(a_ref, gb0_ref, gb1_ref, gbb_ref, ma_ref, x_ref, wao_ref, wo_ref, gf_ref,
                rw_ref, rb_ref, h_ref, uf_ref, lg_ref):
    half = D_MODEL // 2
    yb = jnp.dot(a_ref[...], wao_ref[...], preferred_element_type=F32)
    g0 = jax.nn.sigmoid(gb0_ref[...].astype(F32) + gbb_ref[:, 0:half])
    g1 = jax.nn.sigmoid(gb1_ref[...].astype(F32) + gbb_ref[:, half:D_MODEL])
    m0 = ma_ref[:, 0:half].astype(F32) + g0 * yb[:, 0:half]
    m1 = ma_ref[:, half:D_MODEL].astype(F32) + g1 * yb[:, half:D_MODEL]
    hh = (jnp.dot(m0.astype(BF16), wo_ref[0:half, :], preferred_element_type=F32)
          + jnp.dot(m1.astype(BF16), wo_ref[half:D_MODEL, :], preferred_element_type=F32))
    h = x_ref[...] + hh
    h_ref[...] = h
    ms = jnp.mean(h * h, axis=-1, keepdims=True)
    u = h * lax.rsqrt(ms + NORM_EPS) * gf_ref[...]
    for s in range(ROW_TILES):
        uf_ref[pl.ds(s, TM_MIX, stride=ROW_TILES), :] = u[:, s * LANES:(s + 1) * LANES]
    lg_ref[...] = jnp.dot(u, rw_ref[...], preferred_element_type=F32,
                          precision=lax.Precision.HIGHEST) + rb_ref[...]


def _mix(attn_o, proj, gate_b_b, m_a, x2, wao_bf, wo_bf, g_ffn, router_w, router_b):
    T = x2.shape[0]
    gcol = (COL_G + D_MODEL) // CONV_DIM
    const = lambda i: (0, 0)
    row = lambda i: (i, 0)
    return pl.pallas_call(
        _mix_kernel,
        grid=(T // TM_MIX,),
        in_specs=[
            pl.BlockSpec((TM_MIX, ATTN_DIM), row),
            pl.BlockSpec((TM_MIX, CONV_DIM), lambda i: (i, gcol)),
            pl.BlockSpec((TM_MIX, CONV_DIM), lambda i: (i, gcol + 1)),
            pl.BlockSpec((1, D_MODEL), const),
            pl.BlockSpec((TM_MIX, D_MODEL), row),
            pl.BlockSpec((TM_MIX, D_MODEL), row),
            pl.BlockSpec((ATTN_DIM, D_MODEL), const),
            pl.BlockSpec((D_MODEL, D_MODEL), const),
            pl.BlockSpec((1, D_MODEL), const),
            pl.BlockSpec((D_MODEL, N_EXPERTS), const),
            pl.BlockSpec((1, N_EXPERTS), const),
        ],
        out_specs=[
            pl.BlockSpec((TM_MIX, D_MODEL), row),
            pl.BlockSpec((TM_MIX * ROW_TILES, LANES), row),
            pl.BlockSpec((TM_MIX, N_EXPERTS), row),
        ],
        out_shape=[
            jax.ShapeDtypeStruct((T, D_MODEL), F32),
            jax.ShapeDtypeStruct((T * ROW_TILES, LANES), F32),
            jax.ShapeDtypeStruct((T, N_EXPERTS), F32),
        ],
        compiler_params=_cparams(("parallel",)),
    )(attn_o, proj, proj, gate_b_b, m_a, x2, wao_bf, wo_bf, g_ffn, router_w, router_b)


def _route_kernel(lg_ref, idx_ref, w_ref, rank_ref, cnt_ref, carry_ref):
    i = pl.program_id(0)

    @pl.when(i == 0)
    def _():
        carry_ref[...] = jnp.zeros(carry_ref.shape, F32)

    vals = lg_ref[...]
    lane = lax.broadcasted_iota(jnp.int32, (TR, N_EXPERTS), 1)
    sel = jnp.zeros((TR, N_EXPERTS), F32)
    tops, idxs, hots = [], [], []
    for _ in range(TOP_K):
        m = jnp.max(vals, axis=-1, keepdims=True)
        ix = jnp.min(jnp.where(vals == m, lane, N_EXPERTS), axis=-1, keepdims=True)
        hot = lane == ix
        vals = jnp.where(hot, -jnp.inf, vals)
        sel = sel + hot.astype(F32)
        tops.append(m)
        idxs.append(ix)
        hots.append(hot)
    es = [jnp.exp(t - tops[0]) for t in tops]
    den = es[0] + es[1] + es[2] + es[3]

    r_i = lax.broadcasted_iota(jnp.int32, (TR, TR), 0)
    c_i = lax.broadcasted_iota(jnp.int32, (TR, TR), 1)
    tri = jnp.where(c_i < r_i, 1.0, 0.0).astype(BF16)
    before = jnp.dot(tri, sel.astype(BF16), preferred_element_type=F32) + carry_ref[...]
    carry_ref[...] = carry_ref[...] + jnp.sum(sel, axis=0, keepdims=True)
    cnt_ref[...] = carry_ref[...]

    lane4 = lax.broadcasted_iota(jnp.int32, (TR, TOP_K), 1)
    idx_o = jnp.zeros((TR, TOP_K), jnp.int32)
    w_o = jnp.zeros((TR, TOP_K), F32)
    rk_o = jnp.zeros((TR, TOP_K), F32)
    for k in range(TOP_K):
        rk = jnp.sum(jnp.where(hots[k], before, 0.0), axis=-1, keepdims=True)
        idx_o = jnp.where(lane4 == k, idxs[k], idx_o)
        w_o = jnp.where(lane4 == k, es[k] / den, w_o)
        rk_o = jnp.where(lane4 == k, rk, rk_o)
    idx_ref[...] = idx_o
    w_ref[...] = w_o
    rank_ref[...] = rk_o.astype(jnp.int32)


def _route(logits):
    T = logits.shape[0]
    row = lambda i: (i, 0)
    return pl.pallas_call(
        _route_kernel,
        grid=(T // TR,),
        in_specs=[pl.BlockSpec((TR, N_EXPERTS), row)],
        out_specs=[
            pl.BlockSpec((TR, TOP_K), row),
            pl.BlockSpec((TR, TOP_K), row),
            pl.BlockSpec((TR, TOP_K), row),
            pl.BlockSpec((1, N_EXPERTS), lambda i: (0, 0)),
        ],
        out_shape=[
            jax.ShapeDtypeStruct((T, TOP_K), jnp.int32),
            jax.ShapeDtypeStruct((T, TOP_K), F32),
            jax.ShapeDtypeStruct((T, TOP_K), jnp.int32),
            jax.ShapeDtypeStruct((1, N_EXPERTS), F32),
        ],
        scratch_shapes=[pltpu.VMEM((1, N_EXPERTS), F32)],
        compiler_params=_cparams(("arbitrary",)),
    )(logits)


def _dispatch_kernel(dest_ref, u_hbm, xs_in_hbm, xs_hbm, sem):
    del xs_in_hbm
    i = pl.program_id(0)
    n = TB * TOP_K

    def copy(r):
        src = pl.multiple_of((i * TB + r // TOP_K) * ROW_TILES, ROW_TILES)
        dst = pl.multiple_of(dest_ref[r] * ROW_TILES, ROW_TILES)
        return pltpu.make_async_copy(u_hbm.at[pl.ds(src, ROW_TILES)], xs_hbm.at[pl.ds(dst, ROW_TILES)], sem)

    def issue(r, carry):
        copy(r).start()
        return carry

    def drain(r, carry):
        copy(r).wait()
        return carry

    lax.fori_loop(0, n, issue, 0)
    lax.fori_loop(0, n, drain, 0)


def _dispatch(dest_flat, uf, n_rows):
    T = uf.shape[0] // ROW_TILES
    xs0 = jnp.zeros((n_rows * ROW_TILES, LANES), F32)
    return pl.pallas_call(
        _dispatch_kernel,
        grid=(T // TB,),
        in_specs=[
            pl.BlockSpec((TB * TOP_K,), lambda i: (i,), memory_space=pltpu.SMEM),
            pl.BlockSpec(memory_space=pl.ANY),
            pl.BlockSpec(memory_space=pl.ANY),
        ],
        out_specs=pl.BlockSpec(memory_space=pl.ANY),
        out_shape=jax.ShapeDtypeStruct((n_rows * ROW_TILES, LANES), F32),
        scratch_shapes=[pltpu.SemaphoreType.DMA],
        input_output_aliases={2: 0},
        compiler_params=_cparams(("arbitrary",)),
    )(dest_flat, uf, xs0)


def _ffn_gu_kernel(be_ref, nv_ref, xs_ref, wg_ref, wu_ref, bg_ref, bu_ref, act_ref):
    b = pl.program_id(1)

    @pl.when(b < nv_ref[0])
    def _():
        x = jnp.concatenate(
            [xs_ref[pl.ds(s, M_BLK, stride=ROW_TILES), :].astype(BF16) for s in range(ROW_TILES)], axis=-1)
        gate = jnp.dot(x, wg_ref[0], preferred_element_type=F32) + bg_ref[0]
        up = jnp.dot(x, wu_ref[0], preferred_element_type=F32) + bu_ref[0]
        gate = jnp.minimum(gate, SWIGLU_LIMIT)
        up = jnp.clip(up, -SWIGLU_LIMIT, SWIGLU_LIMIT)
        act = (up + 1.0) * gate * jax.nn.sigmoid(SWIGLU_ALPHA * gate)
        act_ref[...] = act.astype(BF16)

    @pl.when(b >= nv_ref[0])
    def _():
        act_ref[...] = jnp.zeros(act_ref.shape, BF16)


def _ffn_down_kernel(be_ref, nv_ref, act_ref, wd_ref, bd_ref, y_ref):
    b = pl.program_id(0)

    @pl.when(b < nv_ref[0])
    def _():
        y = jnp.dot(act_ref[...], wd_ref[0], preferred_element_type=F32) + bd_ref[0]
        for s in range(ROW_TILES):
            y_ref[pl.ds(s, M_BLK, stride=ROW_TILES), :] = y[:, s * LANES:(s + 1) * LANES]

    @pl.when(b >= nv_ref[0])
    def _():
        y_ref[...] = jnp.zeros(y_ref.shape, F32)


def _blk(b, nv):
    return jnp.minimum(b, nv[0] - 1)


def _ffn_gu(block_e, nvalid, xs, wg, wu, bg, bu):
    n_rows = xs.shape[0] // ROW_TILES
    nb = n_rows // M_BLK
    grid_spec = pltpu.PrefetchScalarGridSpec(
        num_scalar_prefetch=2,
        grid=(D_FF // TF, nb),
        in_specs=[
            pl.BlockSpec((M_BLK * ROW_TILES, LANES), lambda n, b, be, nv: (_blk(b, nv), 0)),
            pl.BlockSpec((1, D_MODEL, TF), lambda n, b, be, nv: (be[_blk(b, nv)], 0, n)),
            pl.BlockSpec((1, D_MODEL, TF), lambda n, b, be, nv: (be[_blk(b, nv)], 0, n)),
            pl.BlockSpec((1, 1, TF), lambda n, b, be, nv: (be[_blk(b, nv)], 0, n)),
            pl.BlockSpec((1, 1, TF), lambda n, b, be, nv: (be[_blk(b, nv)], 0, n)),
        ],
        out_specs=pl.BlockSpec((M_BLK, TF), lambda n, b, be, nv: (b, n)),
    )
    return pl.pallas_call(
        _ffn_gu_kernel,
        grid_spec=grid_spec,
        out_shape=jax.ShapeDtypeStruct((n_rows, D_FF), BF16),
        compiler_params=_cparams(("arbitrary", "arbitrary")),
    )(block_e, nvalid, xs, wg, wu, bg, bu)


def _ffn_down(block_e, nvalid, act, wd, bd):
    n_rows = act.shape[0]
    nb = n_rows // M_BLK
    grid_spec = pltpu.PrefetchScalarGridSpec(
        num_scalar_prefetch=2,
        grid=(nb,),
        in_specs=[
            pl.BlockSpec((M_BLK, D_FF), lambda b, be, nv: (_blk(b, nv), 0)),
            pl.BlockSpec((1, D_FF, D_MODEL), lambda b, be, nv: (be[_blk(b, nv)], 0, 0)),
            pl.BlockSpec((1, 1, D_MODEL), lambda b, be, nv: (be[_blk(b, nv)], 0, 0)),
        ],
        out_specs=pl.BlockSpec((M_BLK * ROW_TILES, LANES), lambda b, be, nv: (b, 0)),
    )
    return pl.pallas_call(
        _ffn_down_kernel,
        grid_spec=grid_spec,
        out_shape=jax.ShapeDtypeStruct((n_rows * ROW_TILES, LANES), F32),
        compiler_params=_cparams(("arbitrary",)),
    )(block_e, nvalid, act, wd, bd)


def _combine_kernel(dest_ref, ys_hbm, h_ref, w_ref, gfin_ref, o_ref, buf, hbuf, sem):
    n = TB * TOP_K

    def copy(r):
        src = pl.multiple_of(dest_ref[r] * ROW_TILES, ROW_TILES)
        dst = pl.multiple_of(r * ROW_TILES, ROW_TILES)
        return pltpu.make_async_copy(ys_hbm.at[pl.ds(src, ROW_TILES)], buf.at[pl.ds(dst, ROW_TILES)], sem)

    def issue(r, carry):
        copy(r).start()
        return carry

    def drain(r, carry):
        copy(r).wait()
        return carry

    lax.fori_loop(0, n, issue, 0)
    lax.fori_loop(0, n, drain, 0)

    w = w_ref[...]
    for s in range(ROW_TILES):
        hs = h_ref[:, s * LANES:(s + 1) * LANES]
        for k in range(TOP_K):
            hs = hs + w[:, k:k + 1] * buf[pl.ds(k * TB * ROW_TILES + s, TB, stride=ROW_TILES), :]
        hbuf[:, s * LANES:(s + 1) * LANES] = hs
    h = hbuf[...]
    ms = jnp.mean(h * h, axis=-1, keepdims=True)
    o_ref[...] = h * lax.rsqrt(ms + NORM_EPS) * gfin_ref[...]


def _combine(dest_kmajor, ys, h1, wts, g_final):
    T = h1.shape[0]
    row = lambda i: (i, 0)
    return pl.pallas_call(
        _combine_kernel,
        grid=(T // TB,),
        in_specs=[
            pl.BlockSpec((TB * TOP_K,), lambda i: (i,), memory_space=pltpu.SMEM),
            pl.BlockSpec(memory_space=pl.ANY),
            pl.BlockSpec((TB, D_MODEL), row),
            pl.BlockSpec((TB, TOP_K), row),
            pl.BlockSpec((1, D_MODEL), lambda i: (0, 0)),
        ],
        out_specs=pl.BlockSpec((TB, D_MODEL), row),
        out_shape=jax.ShapeDtypeStruct((T, D_MODEL), F32),
        scratch_shapes=[pltpu.VMEM((TB * TOP_K * ROW_TILES, LANES), F32),
                        pltpu.VMEM((TB, D_MODEL), F32), pltpu.SemaphoreType.DMA],
        compiler_params=_cparams(("arbitrary",)),
    )(dest_kmajor, ys, h1, wts, g_final)


def _rope_tables(positions):
    half = ROT_DIM // 2
    inv = ROPE_THETA ** (-jnp.arange(0, ROT_DIM, 2, dtype=F32) / ROT_DIM)
    ang = positions.astype(F32).reshape(-1, 1) * inv
    cos, sin = jnp.cos(ang), jnp.sin(ang)
    T = ang.shape[0]
    rest = HEAD_DIM - ROT_DIM
    c64 = jnp.concatenate([cos, cos, jnp.ones((T, rest), F32)], axis=-1)
    s1_64 = jnp.concatenate([-sin, jnp.zeros((T, HEAD_DIM - half), F32)], axis=-1)
    s2_64 = jnp.concatenate([jnp.zeros((T, half), F32), sin, jnp.zeros((T, rest), F32)], axis=-1)
    dup = lambda t: jnp.concatenate([t, t], axis=-1)
    return dup(c64), dup(s1_64), dup(s2_64)


def kernel(x, positions, norm_mix_g, w_in, gate_b, conv_dw_w, conv_dw_b, conv_ln_g, conv_ln_b, w_conv_out, lambda_q1, lambda_k1, lambda_q2, lambda_k2, attn_subln_g, w_attn_out, w_out, norm_ffn_g, router_w, router_b, w_gu, b_gu, w_down, b_down, norm_final_g):
    B, S, D = x.shape
    T = B * S
    x2 = x.reshape(T, D)
    ctab, s1tab, s2tab = _rope_tables(positions)

    proj = _inproj(x2, norm_mix_g[0].reshape(1, D), w_in[0].astype(BF16), ctab, s1tab, s2tab)
    m_a = _conv_branch(proj, B, S, conv_dw_w[0], conv_dw_b[0].reshape(1, -1),
                       conv_ln_g[0].reshape(1, -1), conv_ln_b[0].reshape(1, -1),
                       w_conv_out[0].astype(BF16), gate_b[0, :D].reshape(1, D))
    lam_vecs = jnp.stack([lambda_q1[0], lambda_k1[0], lambda_q2[0], lambda_k2[0]])
    attn_o = _attention(proj, B, S, lam_vecs, attn_subln_g[0].reshape(1, -1))
    h1, uf, logits = _mix(attn_o, proj, gate_b[0, D:].reshape(1, D), m_a, x2,
                          w_attn_out[0].astype(BF16), w_out[0].astype(BF16),
                          norm_ffn_g[0].reshape(1, D), router_w[0], router_b[0].reshape(1, -1))

    idx, wts, rank, counts = _route(logits)
    counts = counts[0].astype(jnp.int32)
    padded = (counts + M_BLK - 1) // M_BLK * M_BLK
    pad_end = jnp.cumsum(padded)
    pad_start = pad_end - padded
    dest = pad_start[idx] + rank
    n_blocks = (T * TOP_K) // M_BLK + N_EXPERTS
    n_rows = n_blocks * M_BLK
    block_e = jnp.minimum(
        jnp.searchsorted(pad_end, jnp.arange(n_blocks, dtype=jnp.int32) * M_BLK, side="right"),
        N_EXPERTS - 1).astype(jnp.int32)
    nvalid = (pad_end[-1:] // M_BLK).astype(jnp.int32)

    xs = _dispatch(dest.reshape(-1), uf, n_rows)
    wg = w_gu[0, :, :, 0::2].astype(BF16)
    wu = w_gu[0, :, :, 1::2].astype(BF16)
    bg = b_gu[0, :, 0::2].reshape(N_EXPERTS, 1, D_FF)
    bu = b_gu[0, :, 1::2].reshape(N_EXPERTS, 1, D_FF)
    act = _ffn_gu(block_e, nvalid, xs, wg, wu, bg, bu)
    ys = _ffn_down(block_e, nvalid, act, w_down[0].astype(BF16), b_down[0].reshape(N_EXPERTS, 1, D))

    dest_km = dest.reshape(T // TB, TB, TOP_K).transpose(0, 2, 1).reshape(-1)
    out = _combine(dest_km, ys, h1, wts, norm_final_g.reshape(1, D))
    return out.reshape(B, S, D)
```

```python
import functools

import jax
import jax.numpy as jnp
from jax import lax
from jax.experimental import pallas as pl
from jax.experimental.pallas import tpu as pltpu

F32 = jnp.float32
BF16 = jnp.bfloat16

D_MODEL = 2048
CHUNK = 64
CONV_DIM = 1024
CONV_WIDTH = 31
N_HEADS = 8
HEAD_DIM = 64
ATTN_DIM = N_HEADS * 2 * HEAD_DIM
ROT_DIM = HEAD_DIM // 4
ROPE_THETA = 500000.0
N_BRANCH = 2
IN_COLS = 2 * CONV_DIM + 3 * ATTN_DIM + N_BRANCH * D_MODEL
N_EXPERTS = 32
TOP_K = 4
D_FF = 2048
SWIGLU_LIMIT = 7.0
SWIGLU_ALPHA = 1.702
NORM_EPS = 1e-6
LN_EPS = 1e-5
LAM_INIT = 0.2

LANES = 128
VMEM_LIMIT = 56 * 1024 * 1024

COL_Q = 2 * CONV_DIM
COL_K = COL_Q + ATTN_DIM
COL_V = COL_K + ATTN_DIM
COL_G = COL_V + ATTN_DIM

TM_IN = 1024
TN_IN = 1024
TS_CONV = 256
HALO = 32
TQ = 512
TK = 512
RG = 128
TM_MIX = 256
TR = 512
TB = 256
M_BLK = 512
TF = 1024
ROW_TILES = D_MODEL // LANES
PERM_W = 256
TR_PREP = 2048
TC_PREP = 512


def _cparams(sem):
    return pltpu.CompilerParams(dimension_semantics=sem, vmem_limit_bytes=VMEM_LIMIT)


def _inproj_kernel(x_ref, g_ref, w_ref, c_ref, s1_ref, s2_ref, o_ref, u_ref):
    j = pl.program_id(1)

    @pl.when(j == 0)
    def _():
        x = x_ref[...]
        ms = jnp.mean(x * x, axis=-1, keepdims=True)
        u_ref[...] = (x * lax.rsqrt(ms + NORM_EPS) * g_ref[...]).astype(BF16)

    acc = jnp.dot(u_ref[...], w_ref[...], preferred_element_type=F32)
    q_tile = COL_Q // TN_IN
    k_tile = COL_K // TN_IN
    is_rope = jnp.logical_or(j == q_tile, j == k_tile)

    @pl.when(is_rope)
    def _():
        scale = jnp.where(j == q_tile, HEAD_DIM ** -0.5, 1.0).astype(F32)
        c = c_ref[...]
        s1 = s1_ref[...]
        s2 = s2_ref[...]
        for hh in range(TN_IN // LANES):
            t = acc[:, hh * LANES:(hh + 1) * LANES]
            r = (t * c + pltpu.roll(t, LANES - ROT_DIM // 2, 1) * s1
                 + pltpu.roll(t, ROT_DIM // 2, 1) * s2)
            o_ref[:, hh * LANES:(hh + 1) * LANES] = (r * scale).astype(BF16)

    @pl.when(jnp.logical_not(is_rope))
    def _():
        o_ref[...] = acc.astype(BF16)


def _inproj(x2, g, w_bf, ctab, s1tab, s2tab):
    T = x2.shape[0]
    return pl.pallas_call(
        _inproj_kernel,
        grid=(T // TM_IN, IN_COLS // TN_IN),
        in_specs=[
            pl.BlockSpec((TM_IN, D_MODEL), lambda i, j: (i, 0)),
            pl.BlockSpec((1, D_MODEL), lambda i, j: (0, 0)),
            pl.BlockSpec((D_MODEL, TN_IN), lambda i, j: (0, j)),
            pl.BlockSpec((TM_IN, LANES), lambda i, j: (i, 0)),
            pl.BlockSpec((TM_IN, LANES), lambda i, j: (i, 0)),
            pl.BlockSpec((TM_IN, LANES), lambda i, j: (i, 0)),
        ],
        out_specs=pl.BlockSpec((TM_IN, TN_IN), lambda i, j: (i, j)),
        out_shape=jax.ShapeDtypeStruct((T, IN_COLS), BF16),
        scratch_shapes=[pltpu.VMEM((TM_IN, D_MODEL), BF16)],
        compiler_params=_cparams(("parallel", "arbitrary")),
    )(x2, g, w_bf, ctab, s1tab, s2tab)


def _conv_kernel(val_ref, gate_ref, hval_ref, hgate_ref, dww_ref, dwb_ref, lng_ref, lnb_ref,
                 wco_ref, ga0_ref, ga1_ref, gab_ref, o_ref, hbuf, cbuf):
    i = pl.program_id(1)
    ts = TS_CONV
    halo = hval_ref[...].astype(F32) * jax.nn.sigmoid(hgate_ref[...].astype(F32))
    hbuf[0:HALO, :] = jnp.where(i == 0, 0.0, halo)
    hbuf[HALO:HALO + ts, :] = val_ref[...].astype(F32) * jax.nn.sigmoid(gate_ref[...].astype(F32))

    base = HALO - (CONV_WIDTH - 1)

    def lane_chunk(c, carry):
        l0 = pl.multiple_of(c * LANES, LANES)
        acc = jnp.zeros((ts, LANES), F32) + dwb_ref[:, pl.ds(l0, LANES)]
        for j in range(CONV_WIDTH):
            acc = acc + dww_ref[j:j + 1, pl.ds(l0, LANES)] * hbuf[base + j:base + j + ts, pl.ds(l0, LANES)]
        cbuf[:, pl.ds(l0, LANES)] = acc
        return carry

    lax.fori_loop(0, CONV_DIM // LANES, lane_chunk, 0)

    y = cbuf[...]
    mu = jnp.mean(y, axis=-1, keepdims=True)
    yc = y - mu
    var = jnp.mean(yc * yc, axis=-1, keepdims=True)
    yn = yc * lax.rsqrt(var + LN_EPS) * lng_ref[...] + lnb_ref[...]
    a = yn * jax.nn.sigmoid(yn)
    out = jnp.dot(a.astype(BF16), wco_ref[...], preferred_element_type=F32)
    half = D_MODEL // 2
    g0 = jax.nn.sigmoid(ga0_ref[...].astype(F32) + gab_ref[:, 0:half])
    g1 = jax.nn.sigmoid(ga1_ref[...].astype(F32) + gab_ref[:, half:D_MODEL])
    o_ref[:, 0:half] = (out[:, 0:half] * g0).astype(BF16)
    o_ref[:, half:D_MODEL] = (out[:, half:D_MODEL] * g1).astype(BF16)


def _conv_branch(proj, B, S, dww, dwb, lng, lnb, wco_bf, gate_b_a):
    T = B * S
    nts = S // TS_CONV
    hpt = TS_CONV // HALO
    gcol = COL_G // CONV_DIM

    def halo_map(col):
        return lambda b, i: (jnp.maximum((b * nts + i) * hpt - 1, 0), col)

    const = lambda b, i: (0, 0)
    return pl.pallas_call(
        _conv_kernel,
        grid=(B, nts),
        in_specs=[
            pl.BlockSpec((TS_CONV, CONV_DIM), lambda b, i: (b * nts + i, 0)),
            pl.BlockSpec((TS_CONV, CONV_DIM), lambda b, i: (b * nts + i, 1)),
            pl.BlockSpec((HALO, CONV_DIM), halo_map(0)),
            pl.BlockSpec((HALO, CONV_DIM), halo_map(1)),
            pl.BlockSpec((CONV_WIDTH, CONV_DIM), const),
            pl.BlockSpec((1, CONV_DIM), const),
            pl.BlockSpec((1, CONV_DIM), const),
            pl.BlockSpec((1, CONV_DIM), const),
            pl.BlockSpec((CONV_DIM, D_MODEL), const),
            pl.BlockSpec((TS_CONV, CONV_DIM), lambda b, i: (b * nts + i, gcol)),
            pl.BlockSpec((TS_CONV, CONV_DIM), lambda b, i: (b * nts + i, gcol + 1)),
            pl.BlockSpec((1, D_MODEL), const),
        ],
        out_specs=pl.BlockSpec((TS_CONV, D_MODEL), lambda b, i: (b * nts + i, 0)),
        out_shape=jax.ShapeDtypeStruct((T, D_MODEL), BF16),
        scratch_shapes=[pltpu.VMEM((HALO + TS_CONV, CONV_DIM), F32),
                        pltpu.VMEM((TS_CONV, CONV_DIM), F32)],
        compiler_params=_cparams(("parallel", "arbitrary")),
    )(proj, proj, proj, proj, dww, dwb, lng, lnb, wco_bf, proj, proj, gate_b_a)


def _attn_kernel(q_ref, k_ref, v_ref, lam_ref, sg_ref, o_ref, qs_ref, vx_ref, m_ref, acc_ref):
    qi = pl.program_id(2)
    hw = 2 * HEAD_DIM

    @pl.when(qi == 0)
    def _():
        vx_ref[:, 0:hw] = v_ref[...]
        vx_ref[:, hw:2 * hw] = jnp.ones((v_ref.shape[0], hw), BF16)

    q = q_ref[...]
    lane = lax.broadcasted_iota(jnp.int32, (TQ, hw), 1)
    zero = jnp.zeros_like(q)
    qs_ref[0:TQ, :] = jnp.where(lane < HEAD_DIM, q, zero)
    qs_ref[TQ:2 * TQ, :] = jnp.where(lane >= HEAD_DIM, q, zero)
    m_ref[...] = jnp.full(m_ref.shape, -jnp.inf, F32)
    acc_ref[...] = jnp.zeros(acc_ref.shape, F32)

    def step(kc, masked):
        k0 = pl.multiple_of(kc * TK, TK)
        kb = k_ref[pl.ds(k0, TK), :]
        vb = vx_ref[pl.ds(k0, TK), :]
        for g in range(2 * TQ // RG):
            r0 = g * RG
            s = lax.dot_general(qs_ref[r0:r0 + RG, :], kb, (((1,), (1,)), ((), ())),
                                preferred_element_type=F32)
            if masked:
                row = lax.broadcasted_iota(jnp.int32, (RG, TK), 0) + (r0 % TQ)
                col = lax.broadcasted_iota(jnp.int32, (RG, TK), 1)
                s = jnp.where(col // CHUNK <= row // CHUNK, s, -jnp.inf)
            m_old = m_ref[r0:r0 + RG, :]
            m_new = jnp.maximum(m_old, jnp.max(s, axis=-1, keepdims=True))
            alpha = jnp.exp(m_old - m_new)
            p = jnp.exp(s - jnp.tile(m_new, (1, TK // LANES)))
            pv = jnp.dot(p.astype(BF16), vb, preferred_element_type=F32)
            acc_ref[r0:r0 + RG, :] = jnp.tile(alpha, (1, 2)) * acc_ref[r0:r0 + RG, :] + pv
            m_ref[r0:r0 + RG, :] = m_new

    def body(kc, carry):
        step(kc, False)
        return carry

    lax.fori_loop(0, qi, body, 0)
    step(qi, True)

    lam_v = lam_ref[...]
    lam = (jnp.exp(jnp.sum(lam_v[0:1, :] * lam_v[1:2, :], axis=-1, keepdims=True))
           - jnp.exp(jnp.sum(lam_v[2:3, :] * lam_v[3:4, :], axis=-1, keepdims=True)) + LAM_INIT)
    o0 = acc_ref[0:TQ, 0:hw] / acc_ref[0:TQ, hw:2 * hw]
    o1 = acc_ref[TQ:2 * TQ, 0:hw] / acc_ref[TQ:2 * TQ, hw:2 * hw]
    o = o0 - lam * o1
    ms = jnp.mean(o * o, axis=-1, keepdims=True)
    o = o * lax.rsqrt(ms + NORM_EPS) * sg_ref[...] * (1.0 - LAM_INIT)
    o_ref[...] = o.astype(BF16)


def _attention(proj, B, S, lam_vecs, subln_g):
    T = B * S
    nq = S // TQ
    hw = 2 * HEAD_DIM
    qc, kc, vc = COL_Q // hw, COL_K // hw, COL_V // hw
    return pl.pallas_call(
        _attn_kernel,
        grid=(B, N_HEADS, nq),
        in_specs=[
            pl.BlockSpec((TQ, hw), lambda b, h, i: (b * nq + i, qc + h)),
            pl.BlockSpec((S, hw), lambda b, h, i: (b, kc + h)),
            pl.BlockSpec((S, hw), lambda b, h, i: (b, vc + h)),
            pl.BlockSpec((4, HEAD_DIM), lambda b, h, i: (0, 0)),
            pl.BlockSpec((1, hw), lambda b, h, i: (0, 0)),
        ],
        out_specs=pl.BlockSpec((TQ, hw), lambda b, h, i: (b * nq + i, h)),
        out_shape=jax.ShapeDtypeStruct((T, ATTN_DIM), BF16),
        scratch_shapes=[pltpu.VMEM((2 * TQ, hw), BF16),
                        pltpu.VMEM((S, 2 * hw), BF16),
                        pltpu.VMEM((2 * TQ, LANES), F32),
                        pltpu.VMEM((2 * TQ, 2 * hw), F32)],
        compiler_params=_cparams(("parallel", "parallel", "arbitrary")),
    )(proj, proj, proj, lam_vecs, subln_g)


def _mix_kernel(a_ref, gb0_ref, gb1_ref, gbb_ref, ma_ref, x_ref, wao_ref, wo_ref, gf_ref,
                rw_ref, rb_ref, h_ref, uf_ref, lg_ref):
    half = D_MODEL // 2
    yb = jnp.dot(a_ref[...], wao_ref[...], preferred_element_type=F32)
    g0 = jax.nn.sigmoid(gb0_ref[...].astype(F32) + gbb_ref[:, 0:half])
    g1 = jax.nn.sigmoid(gb1_ref[...].astype(F32) + gbb_ref[:, half:D_MODEL])
    m0 = ma_ref[:, 0:half].astype(F32) + g0 * yb[:, 0:half]
    m1 = ma_ref[:, half:D_MODEL].astype(F32) + g1 * yb[:, half:D_MODEL]
    hh = (jnp.dot(m0.astype(BF16), wo_ref[0:half, :], preferred_element_type=F32)
          + jnp.dot(m1.astype(BF16), wo_ref[half:D_MODEL, :], preferred_element_type=F32))
    h = x_ref[...] + hh
    h_ref[...] = h
    ms = jnp.mean(h * h, axis=-1, keepdims=True)
    u = h * lax.rsqrt(ms + NORM_EPS) * gf_ref[...]
    for s in range(ROW_TILES):
        uf_ref[pl.ds(s, TM_MIX, stride=ROW_TILES), :] = u[:, s * LANES:(s + 1) * LANES]
    lg_ref[...] = jnp.dot(u, rw_ref[...], preferred_element_type=F32,
                          precision=lax.Precision.HIGHEST) + rb_ref[...]


def _mix(attn_o, proj, gate_b_b, m_a, x2, wao_bf, wo_bf, g_ffn, router_w, router_b):
    T = x2.shape[0]
    gcol = (COL_G + D_MODEL) // CONV_DIM
    const = lambda i: (0, 0)
    row = lambda i: (i, 0)
    return pl.pallas_call(
        _mix_kernel,
        grid=(T // TM_MIX,),
        in_specs=[
            pl.BlockSpec((TM_MIX, ATTN_DIM), row),
            pl.BlockSpec((TM_MIX, CONV_DIM), lambda i: (i, gcol)),
            pl.BlockSpec((TM_MIX, CONV_DIM), lambda i: (i, gcol + 1)),
            pl.BlockSpec((1, D_MODEL), const),
            pl.BlockSpec((TM_MIX, D_MODEL), row),
            pl.BlockSpec((TM_MIX, D_MODEL), row),
            pl.BlockSpec((ATTN_DIM, D_MODEL), const),
            pl.BlockSpec((D_MODEL, D_MODEL), const),
            pl.BlockSpec((1, D_MODEL), const),
            pl.BlockSpec((D_MODEL, N_EXPERTS), const),
            pl.BlockSpec((1, N_EXPERTS), const),
        ],
        out_specs=[
            pl.BlockSpec((TM_MIX, D_MODEL), row),
            pl.BlockSpec((TM_MIX * ROW_TILES, LANES), row),
            pl.BlockSpec((TM_MIX, N_EXPERTS), row),
        ],
        out_shape=[
            jax.ShapeDtypeStruct((T, D_MODEL), F32),
            jax.ShapeDtypeStruct((T * ROW_TILES, LANES), F32),
            jax.ShapeDtypeStruct((T, N_EXPERTS), F32),
        ],
        compiler_params=_cparams(("parallel",)),
    )(attn_o, proj, proj, gate_b_b, m_a, x2, wao_bf, wo_bf, g_ffn, router_w, router_b)


def _route_kernel(lg_ref, idx_ref, w_ref, rank_ref, cnt_ref, carry_ref):
    i = pl.program_id(0)

    @pl.when(i == 0)
    def _():
        carry_ref[...] = jnp.zeros(carry_ref.shape, F32)

    vals = lg_ref[...]
    lane = lax.broadcasted_iota(jnp.int32, (TR, N_EXPERTS), 1)
    sel = jnp.zeros((TR, N_EXPERTS), F32)
    tops, idxs, hots = [], [], []
    for _ in range(TOP_K):
        m = jnp.max(vals, axis=-1, keepdims=True)
        ix = jnp.min(jnp.where(vals == m, lane, N_EXPERTS), axis=-1, keepdims=True)
        hot = lane == ix
        vals = jnp.where(hot, -jnp.inf, vals)
        sel = sel + hot.astype(F32)
        tops.append(m)
        idxs.append(ix)
        hots.append(hot)
    es = [jnp.exp(t - tops[0]) for t in tops]
    den = es[0] + es[1] + es[2] + es[3]

    r_i = lax.broadcasted_iota(jnp.int32, (TR, TR), 0)
    c_i = lax.broadcasted_iota(jnp.int32, (TR, TR), 1)
    tri = jnp.where(c_i < r_i, 1.0, 0.0).astype(BF16)
    before = jnp.dot(tri, sel.astype(BF16), preferred_element_type=F32) + carry_ref[...]
    carry_ref[...] = carry_ref[...] + jnp.sum(sel, axis=0, keepdims=True)
    cnt_ref[...] = carry_ref[...]

    lane4 = lax.broadcasted_iota(jnp.int32, (TR, TOP_K), 1)
    idx_o = jnp.zeros((TR, TOP_K), jnp.int32)
    w_o = jnp.zeros((TR, TOP_K), F32)
    rk_o = jnp.zeros((TR, TOP_K), F32)
    for k in range(TOP_K):
        rk = jnp.sum(jnp.where(hots[k], before, 0.0), axis=-1, keepdims=True)
        idx_o = jnp.where(lane4 == k, idxs[k], idx_o)
        w_o = jnp.where(lane4 == k, es[k] / den, w_o)
        rk_o = jnp.where(lane4 == k, rk, rk_o)
    idx_ref[...] = idx_o
    w_ref[...] = w_o
    rank_ref[...] = rk_o.astype(jnp.int32)


def _route(logits):
    T = logits.shape[0]
    row = lambda i: (i, 0)
    return pl.pallas_call(
        _route_kernel,
        grid=(T // TR,),
        in_specs=[pl.BlockSpec((TR, N_EXPERTS), row)],
        out_specs=[
            pl.BlockSpec((TR, TOP_K), row),
            pl.BlockSpec((TR, TOP_K), row),
            pl.BlockSpec((TR, TOP_K), row),
            pl.BlockSpec((1, N_EXPERTS), lambda i: (0, 0)),
        ],
        out_shape=[
            jax.ShapeDtypeStruct((T, TOP_K), jnp.int32),
            jax.ShapeDtypeStruct((T, TOP_K), F32),
            jax.ShapeDtypeStruct((T, TOP_K), jnp.int32),
            jax.ShapeDtypeStruct((1, N_EXPERTS), F32),
        ],
        scratch_shapes=[pltpu.VMEM((1, N_EXPERTS), F32)],
        compiler_params=_cparams(("arbitrary",)),
    )(logits)


def _dispatch_kernel(dest_ref, u_ref, xs_in_hbm, xs_hbm, sem):
    del xs_in_hbm
    n = TB * TOP_K

    def copy(r):
        src = pl.multiple_of((r // TOP_K) * ROW_TILES, ROW_TILES)
        dst = pl.multiple_of(dest_ref[r] * ROW_TILES, ROW_TILES)
        return pltpu.make_async_copy(u_ref.at[pl.ds(src, ROW_TILES)], xs_hbm.at[pl.ds(dst, ROW_TILES)], sem)

    def issue(r, carry):
        copy(r).start()
        return carry

    def drain(r, carry):
        copy(r).wait()
        return carry

    lax.fori_loop(0, n, issue, 0, unroll=8)
    lax.fori_loop(0, n, drain, 0, unroll=8)


def _dispatch(dest_flat, uf, n_rows):
    T = uf.shape[0] // ROW_TILES
    xs0 = jnp.zeros((n_rows * ROW_TILES, LANES), F32)
    return pl.pallas_call(
        _dispatch_kernel,
        grid=(T // TB,),
        in_specs=[
            pl.BlockSpec((TB * TOP_K,), lambda i: (i,), memory_space=pltpu.SMEM),
            pl.BlockSpec((TB * ROW_TILES, LANES), lambda i: (i, 0)),
            pl.BlockSpec(memory_space=pl.ANY),
        ],
        out_specs=pl.BlockSpec(memory_space=pl.ANY),
        out_shape=jax.ShapeDtypeStruct((n_rows * ROW_TILES, LANES), F32),
        scratch_shapes=[pltpu.SemaphoreType.DMA],
        input_output_aliases={2: 0},
        compiler_params=_cparams(("arbitrary",)),
    )(dest_flat, uf, xs0)


def _split_gu_kernel(w_ref, wg_ref, wu_ref):
    r_i = lax.broadcasted_iota(jnp.int32, (PERM_W, PERM_W), 0)
    c_i = lax.broadcasted_iota(jnp.int32, (PERM_W, PERM_W), 1)
    src = jnp.where(c_i < PERM_W // 2, 2 * c_i, 2 * (c_i - PERM_W // 2) + 1)
    perm = jnp.where(r_i == src, 1.0, 0.0).astype(BF16)
    hw = PERM_W // 2
    for c in range(TC_PREP // PERM_W):
        chunk = w_ref[:, c * PERM_W:(c + 1) * PERM_W].astype(BF16)
        res = jnp.dot(chunk, perm, preferred_element_type=F32).astype(BF16)
        wg_ref[:, c * hw:(c + 1) * hw] = res[:, 0:hw]
        wu_ref[:, c * hw:(c + 1) * hw] = res[:, hw:PERM_W]


def _split_gu(w_gu2):
    rows, cols = w_gu2.shape
    out = jax.ShapeDtypeStruct((rows, cols // 2), BF16)
    return pl.pallas_call(
        _split_gu_kernel,
        grid=(rows // TR_PREP, cols // TC_PREP),
        in_specs=[pl.BlockSpec((TR_PREP, TC_PREP), lambda i, j: (i, j))],
        out_specs=[pl.BlockSpec((TR_PREP, TC_PREP // 2), lambda i, j: (i, j)),
                   pl.BlockSpec((TR_PREP, TC_PREP // 2), lambda i, j: (i, j))],
        out_shape=[out, out],
        compiler_params=_cparams(("parallel", "parallel")),
    )(w_gu2)


def _ffn_gu_kernel(be_ref, nv_ref, xs_ref, wg_ref, wu_ref, bg_ref, bu_ref, act_ref):
    b = pl.program_id(1)

    @pl.when(b < nv_ref[0])
    def _():
        x = jnp.concatenate(
            [xs_ref[pl.ds(s, M_BLK, stride=ROW_TILES), :].astype(BF16) for s in range(ROW_TILES)], axis=-1)
        gate = jnp.dot(x, wg_ref[0], preferred_element_type=F32) + bg_ref[0]
        up = jnp.dot(x, wu_ref[0], preferred_element_type=F32) + bu_ref[0]
        gate = jnp.minimum(gate, SWIGLU_LIMIT)
        up = jnp.clip(up, -SWIGLU_LIMIT, SWIGLU_LIMIT)
        act = (up + 1.0) * gate * jax.nn.sigmoid(SWIGLU_ALPHA * gate)
        act_ref[...] = act.astype(BF16)

    @pl.when(b >= nv_ref[0])
    def _():
        act_ref[...] = jnp.zeros(act_ref.shape, BF16)


def _ffn_down_kernel(be_ref, nv_ref, act_ref, wd_ref, bd_ref, y_ref):
    b = pl.program_id(0)

    @pl.when(b < nv_ref[0])
    def _():
        y = jnp.dot(act_ref[...], wd_ref[0], preferred_element_type=F32) + bd_ref[0]
        for s in range(ROW_TILES):
            y_ref[pl.ds(s, M_BLK, stride=ROW_TILES), :] = y[:, s * LANES:(s + 1) * LANES]

    @pl.when(b >= nv_ref[0])
    def _():
        y_ref[...] = jnp.zeros(y_ref.shape, F32)


def _blk(b, nv):
    return jnp.minimum(b, nv[0] - 1)


def _ffn_gu(block_e, nvalid, xs, wg, wu, bg, bu):
    n_rows = xs.shape[0] // ROW_TILES
    nb = n_rows // M_BLK
    grid_spec = pltpu.PrefetchScalarGridSpec(
        num_scalar_prefetch=2,
        grid=(D_FF // TF, nb),
        in_specs=[
            pl.BlockSpec((M_BLK * ROW_TILES, LANES), lambda n, b, be, nv: (_blk(b, nv), 0)),
            pl.BlockSpec((1, D_MODEL, TF), lambda n, b, be, nv: (be[_blk(b, nv)], 0, n)),
            pl.BlockSpec((1, D_MODEL, TF), lambda n, b, be, nv: (be[_blk(b, nv)], 0, n)),
            pl.BlockSpec((1, 1, TF), lambda n, b, be, nv: (be[_blk(b, nv)], 0, n)),
            pl.BlockSpec((1, 1, TF), lambda n, b, be, nv: (be[_blk(b, nv)], 0, n)),
        ],
        out_specs=pl.BlockSpec((M_BLK, TF), lambda n, b, be, nv: (b, n)),
    )
    return pl.pallas_call(
        _ffn_gu_kernel,
        grid_spec=grid_spec,
        out_shape=jax.ShapeDtypeStruct((n_rows, D_FF), BF16),
        compiler_params=_cparams(("arbitrary", "arbitrary")),
    )(block_e, nvalid, xs, wg, wu, bg, bu)


def _ffn_down(block_e, nvalid, act, wd, bd):
    n_rows = act.shape[0]
    nb = n_rows // M_BLK
    grid_spec = pltpu.PrefetchScalarGridSpec(
        num_scalar_prefetch=2,
        grid=(nb,),
        in_specs=[
            pl.BlockSpec((M_BLK, D_FF), lambda b, be, nv: (_blk(b, nv), 0)),
            pl.BlockSpec((1, D_FF, D_MODEL), lambda b, be, nv: (be[_blk(b, nv)], 0, 0)),
            pl.BlockSpec((1, 1, D_MODEL), lambda b, be, nv: (be[_blk(b, nv)], 0, 0)),
        ],
        out_specs=pl.BlockSpec((M_BLK * ROW_TILES, LANES), lambda b, be, nv: (b, 0)),
    )
    return pl.pallas_call(
        _ffn_down_kernel,
        grid_spec=grid_spec,
        out_shape=jax.ShapeDtypeStruct((n_rows * ROW_TILES, LANES), F32),
        compiler_params=_cparams(("arbitrary",)),
    )(block_e, nvalid, act, wd, bd)


def _combine_kernel(dest_ref, ys_hbm, h_ref, w_ref, gfin_ref, o_ref, buf, hbuf, sem):
    n = TB * TOP_K

    def copy(r):
        src = pl.multiple_of(dest_ref[r] * ROW_TILES, ROW_TILES)
        dst = pl.multiple_of(r * ROW_TILES, ROW_TILES)
        return pltpu.make_async_copy(ys_hbm.at[pl.ds(src, ROW_TILES)], buf.at[pl.ds(dst, ROW_TILES)], sem)

    def issue(r, carry):
        copy(r).start()
        return carry

    def drain(r, carry):
        copy(r).wait()
        return carry

    lax.fori_loop(0, n, issue, 0, unroll=8)
    lax.fori_loop(0, n, drain, 0, unroll=8)

    w = w_ref[...]
    for s in range(ROW_TILES):
        hs = h_ref[:, s * LANES:(s + 1) * LANES]
        for k in range(TOP_K):
            hs = hs + w[:, k:k + 1] * buf[pl.ds(k * TB * ROW_TILES + s, TB, stride=ROW_TILES), :]
        hbuf[:, s * LANES:(s + 1) * LANES] = hs
    h = hbuf[...]
    ms = jnp.mean(h * h, axis=-1, keepdims=True)
    o_ref[...] = h * lax.rsqrt(ms + NORM_EPS) * gfin_ref[...]


def _combine(dest_kmajor, ys, h1, wts, g_final):
    T = h1.shape[0]
    row = lambda i: (i, 0)
    return pl.pallas_call(
        _combine_kernel,
        grid=(T // TB,),
        in_specs=[
            pl.BlockSpec((TB * TOP_K,), lambda i: (i,), memory_space=pltpu.SMEM),
            pl.BlockSpec(memory_space=pl.ANY),
            pl.BlockSpec((TB, D_MODEL), row),
            pl.BlockSpec((TB, TOP_K), row),
            pl.BlockSpec((1, D_MODEL), lambda i: (0, 0)),
        ],
        out_specs=pl.BlockSpec((TB, D_MODEL), row),
        out_shape=jax.ShapeDtypeStruct((T, D_MODEL), F32),
        scratch_shapes=[pltpu.VMEM((TB * TOP_K * ROW_TILES, LANES), F32),
                        pltpu.VMEM((TB, D_MODEL), F32), pltpu.SemaphoreType.DMA],
        compiler_params=_cparams(("arbitrary",)),
    )(dest_kmajor, ys, h1, wts, g_final)


def _rope_tables(positions):
    half = ROT_DIM // 2
    inv = ROPE_THETA ** (-jnp.arange(0, ROT_DIM, 2, dtype=F32) / ROT_DIM)
    ang = positions.astype(F32).reshape(-1, 1) * inv
    cos, sin = jnp.cos(ang), jnp.sin(ang)
    T = ang.shape[0]
    rest = HEAD_DIM - ROT_DIM
    c64 = jnp.concatenate([cos, cos, jnp.ones((T, rest), F32)], axis=-1)
    s1_64 = jnp.concatenate([-sin, jnp.zeros((T, HEAD_DIM - half), F32)], axis=-1)
    s2_64 = jnp.concatenate([jnp.zeros((T, half), F32), sin, jnp.zeros((T, rest), F32)], axis=-1)
    dup = lambda t: jnp.concatenate([t, t], axis=-1)
    return dup(c64), dup(s1_64), dup(s2_64)


def kernel(x, positions, norm_mix_g, w_in, gate_b, conv_dw_w, conv_dw_b, conv_ln_g, conv_ln_b, w_conv_out, lambda_q1, lambda_k1, lambda_q2, lambda_k2, attn_subln_g, w_attn_out, w_out, norm_ffn_g, router_w, router_b, w_gu, b_gu, w_down, b_down, norm_final_g):
    B, S, D = x.shape
    T = B * S
    x2 = x.reshape(T, D)
    ctab, s1tab, s2tab = _rope_tables(positions)

    proj = _inproj(x2, norm_mix_g[0].reshape(1, D), w_in[0].astype(BF16), ctab, s1tab, s2tab)
    m_a = _conv_branch(proj, B, S, conv_dw_w[0], conv_dw_b[0].reshape(1, -1),
                       conv_ln_g[0].reshape(1, -1), conv_ln_b[0].reshape(1, -1),
                       w_conv_out[0].astype(BF16), gate_b[0, :D].reshape(1, D))
    lam_vecs = jnp.stack([lambda_q1[0], lambda_k1[0], lambda_q2[0], lambda_k2[0]])
    attn_o = _attention(proj, B, S, lam_vecs, attn_subln_g[0].reshape(1, -1))
    h1, uf, logits = _mix(attn_o, proj, gate_b[0, D:].reshape(1, D), m_a, x2,
                          w_attn_out[0].astype(BF16), w_out[0].astype(BF16),
                          norm_ffn_g[0].reshape(1, D), router_w[0], router_b[0].reshape(1, -1))

    idx, wts, rank, counts = _route(logits)
    counts = counts[0].astype(jnp.int32)
    padded = (counts + M_BLK - 1) // M_BLK * M_BLK
    pad_end = jnp.cumsum(padded)
    pad_start = pad_end - padded
    dest = pad_start[idx] + rank
    n_blocks = (T * TOP_K) // M_BLK + N_EXPERTS
    n_rows = n_blocks * M_BLK
    block_e = jnp.minimum(
        jnp.searchsorted(pad_end, jnp.arange(n_blocks, dtype=jnp.int32) * M_BLK, side="right"),
        N_EXPERTS - 1).astype(jnp.int32)
    nvalid = (pad_end[-1:] // M_BLK).astype(jnp.int32)

    xs = _dispatch(dest.reshape(-1), uf, n_rows)
    wg, wu = _split_gu(w_gu[0].reshape(N_EXPERTS * D, 2 * D_FF))
    wg = wg.reshape(N_EXPERTS, D, D_FF)
    wu = wu.reshape(N_EXPERTS, D, D_FF)
    bg = b_gu[0, :, 0::2].reshape(N_EXPERTS, 1, D_FF)
    bu = b_gu[0, :, 1::2].reshape(N_EXPERTS, 1, D_FF)
    act = _ffn_gu(block_e, nvalid, xs, wg, wu, bg, bu)
    ys = _ffn_down(block_e, nvalid, act, w_down[0].astype(BF16), b_down[0].reshape(N_EXPERTS, 1, D))

    dest_km = dest.reshape(T // TB, TB, TOP_K).transpose(0, 2, 1).reshape(-1)
    out = _combine(dest_km, ys, h1, wts, norm_final_g.reshape(1, D))
    return out.reshape(B, S, D)
```

```python
import functools

import jax
import jax.numpy as jnp
from jax import lax
from jax.experimental import pallas as pl
from jax.experimental.pallas import tpu as pltpu

F32 = jnp.float32
BF16 = jnp.bfloat16

D_MODEL = 2048
CHUNK = 64
CONV_DIM = 1024
CONV_WIDTH = 31
N_HEADS = 8
HEAD_DIM = 64
ATTN_DIM = N_HEADS * 2 * HEAD_DIM
ROT_DIM = HEAD_DIM // 4
ROPE_THETA = 500000.0
N_BRANCH = 2
IN_COLS = 2 * CONV_DIM + 3 * ATTN_DIM + N_BRANCH * D_MODEL
N_EXPERTS = 32
TOP_K = 4
D_FF = 2048
SWIGLU_LIMIT = 7.0
SWIGLU_ALPHA = 1.702
NORM_EPS = 1e-6
LN_EPS = 1e-5
LAM_INIT = 0.2

LANES = 128
SUBLANES = 8
VMEM_LIMIT = 56 * 1024 * 1024

COL_Q = 2 * CONV_DIM
COL_K = COL_Q + ATTN_DIM
COL_V = COL_K + ATTN_DIM
COL_G = COL_V + ATTN_DIM

TM_IN = 1024
TN_IN = 1024
TS_CONV = 256
HALO = 32
TQ = 512
TK = 512
RG = 128
KV_UNROLL = 4
TM_MIX = 256
TR = 512
TB = 256
CT_COMB = 64
M_BLK = 512
TF = 1024
ROW_TILES = D_MODEL // LANES
PERM_W = 256
TR_PREP = 2048
TC_PREP = 512


def _cparams(sem):
    return pltpu.CompilerParams(dimension_semantics=sem, vmem_limit_bytes=VMEM_LIMIT)


def _inproj_kernel(x_ref, g_ref, w_ref, c_ref, s1_ref, s2_ref, o_ref, u_ref):
    j = pl.program_id(1)

    @pl.when(j == 0)
    def _():
        x = x_ref[...]
        ms = jnp.mean(x * x, axis=-1, keepdims=True)
        u_ref[...] = (x * lax.rsqrt(ms + NORM_EPS) * g_ref[...]).astype(BF16)

    acc = jnp.dot(u_ref[...], w_ref[...], preferred_element_type=F32)
    q_tile = COL_Q // TN_IN
    k_tile = COL_K // TN_IN
    is_rope = jnp.logical_or(j == q_tile, j == k_tile)

    @pl.when(is_rope)
    def _():
        scale = jnp.where(j == q_tile, HEAD_DIM ** -0.5, 1.0).astype(F32)
        c = c_ref[...]
        s1 = s1_ref[...]
        s2 = s2_ref[...]
        for hh in range(TN_IN // LANES):
            t = acc[:, hh * LANES:(hh + 1) * LANES]
            r = (t * c + pltpu.roll(t, LANES - ROT_DIM // 2, 1) * s1
                 + pltpu.roll(t, ROT_DIM // 2, 1) * s2)
            o_ref[:, hh * LANES:(hh + 1) * LANES] = (r * scale).astype(BF16)

    @pl.when(jnp.logical_not(is_rope))
    def _():
        o_ref[...] = acc.astype(BF16)


def _inproj(x2, g, w_bf, ctab, s1tab, s2tab):
    T = x2.shape[0]
    return pl.pallas_call(
        _inproj_kernel,
        grid=(T // TM_IN, IN_COLS // TN_IN),
        in_specs=[
            pl.BlockSpec((TM_IN, D_MODEL), lambda i, j: (i, 0)),
            pl.BlockSpec((1, D_MODEL), lambda i, j: (0, 0)),
            pl.BlockSpec((D_MODEL, TN_IN), lambda i, j: (0, j)),
            pl.BlockSpec((TM_IN, LANES), lambda i, j: (i, 0)),
            pl.BlockSpec((TM_IN, LANES), lambda i, j: (i, 0)),
            pl.BlockSpec((TM_IN, LANES), lambda i, j: (i, 0)),
        ],
        out_specs=pl.BlockSpec((TM_IN, TN_IN), lambda i, j: (i, j)),
        out_shape=jax.ShapeDtypeStruct((T, IN_COLS), BF16),
        scratch_shapes=[pltpu.VMEM((TM_IN, D_MODEL), BF16)],
        compiler_params=_cparams(("parallel", "arbitrary")),
    )(x2, g, w_bf, ctab, s1tab, s2tab)


def _conv_kernel(val_ref, gate_ref, hval_ref, hgate_ref, dww_ref, dwb_ref, lng_ref, lnb_ref,
                 wco_ref, ga0_ref, ga1_ref, gab_ref, o_ref, hbuf, cbuf, shbuf):
    i = pl.program_id(1)
    ts = TS_CONV
    halo = hval_ref[...].astype(F32) * jax.nn.sigmoid(hgate_ref[...].astype(F32))
    hbuf[0:HALO, :] = jnp.where(i == 0, 0.0, halo)
    hbuf[HALO:HALO + ts, :] = val_ref[...].astype(F32) * jax.nn.sigmoid(gate_ref[...].astype(F32))

    base = HALO - (CONV_WIDTH - 1)

    def lane_chunk(c, carry):
        l0 = pl.multiple_of(c * LANES, LANES)
        acc = jnp.zeros((ts, LANES), F32) + dwb_ref[:, pl.ds(l0, LANES)]
        for r in range(SUBLANES):
            taps = [j for j in range(CONV_WIDTH) if (base + j) % SUBLANES == r]
            q_max = max((base + j) // SUBLANES for j in taps)
            n = SUBLANES * q_max + ts
            shbuf[0:n, :] = hbuf[r:r + n, pl.ds(l0, LANES)]
            for j in taps:
                q = (base + j) // SUBLANES
                acc = acc + dww_ref[j:j + 1, pl.ds(l0, LANES)] * shbuf[SUBLANES * q:SUBLANES * q + ts, :]
        cbuf[:, pl.ds(l0, LANES)] = acc
        return carry

    lax.fori_loop(0, CONV_DIM // LANES, lane_chunk, 0)

    y = cbuf[...]
    mu = jnp.mean(y, axis=-1, keepdims=True)
    yc = y - mu
    var = jnp.mean(yc * yc, axis=-1, keepdims=True)
    yn = yc * lax.rsqrt(var + LN_EPS) * lng_ref[...] + lnb_ref[...]
    a = yn * jax.nn.sigmoid(yn)
    out = jnp.dot(a.astype(BF16), wco_ref[...], preferred_element_type=F32)
    half = D_MODEL // 2
    g0 = jax.nn.sigmoid(ga0_ref[...].astype(F32) + gab_ref[:, 0:half])
    g1 = jax.nn.sigmoid(ga1_ref[...].astype(F32) + gab_ref[:, half:D_MODEL])
    o_ref[:, 0:half] = (out[:, 0:half] * g0).astype(BF16)
    o_ref[:, half:D_MODEL] = (out[:, half:D_MODEL] * g1).astype(BF16)


def _conv_branch(proj, B, S, dww, dwb, lng, lnb, wco_bf, gate_b_a):
    T = B * S
    nts = S // TS_CONV
    hpt = TS_CONV // HALO
    gcol = COL_G // CONV_DIM

    def halo_map(col):
        return lambda b, i: (jnp.maximum((b * nts + i) * hpt - 1, 0), col)

    const = lambda b, i: (0, 0)
    return pl.pallas_call(
        _conv_kernel,
        grid=(B, nts),
        in_specs=[
            pl.BlockSpec((TS_CONV, CONV_DIM), lambda b, i: (b * nts + i, 0)),
            pl.BlockSpec((TS_CONV, CONV_DIM), lambda b, i: (b * nts + i, 1)),
            pl.BlockSpec((HALO, CONV_DIM), halo_map(0)),
            pl.BlockSpec((HALO, CONV_DIM), halo_map(1)),
            pl.BlockSpec((CONV_WIDTH, CONV_DIM), const),
            pl.BlockSpec((1, CONV_DIM), const),
            pl.BlockSpec((1, CONV_DIM), const),
            pl.BlockSpec((1, CONV_DIM), const),
            pl.BlockSpec((CONV_DIM, D_MODEL), const),
            pl.BlockSpec((TS_CONV, CONV_DIM), lambda b, i: (b * nts + i, gcol)),
            pl.BlockSpec((TS_CONV, CONV_DIM), lambda b, i: (b * nts + i, gcol + 1)),
            pl.BlockSpec((1, D_MODEL), const),
        ],
        out_specs=pl.BlockSpec((TS_CONV, D_MODEL), lambda b, i: (b * nts + i, 0)),
        out_shape=jax.ShapeDtypeStruct((T, D_MODEL), BF16),
        scratch_shapes=[pltpu.VMEM((HALO + TS_CONV, CONV_DIM), F32),
                        pltpu.VMEM((TS_CONV, CONV_DIM), F32),
                        pltpu.VMEM((HALO + TS_CONV, LANES), F32)],
        compiler_params=_cparams(("parallel", "arbitrary")),
    )(proj, proj, proj, proj, dww, dwb, lng, lnb, wco_bf, proj, proj, gate_b_a)


def _attn_kernel(q_ref, k_ref, v_ref, lam_ref, sg_ref, o_ref, qs_ref, vx_ref, m_ref, acc_ref):
    qi = pl.program_id(2)
    hw = 2 * HEAD_DIM

    @pl.when(qi == 0)
    def _():
        vx_ref[:, 0:hw] = v_ref[...]
        vx_ref[:, hw:2 * hw] = jnp.ones((v_ref.shape[0], hw), BF16)

    q = q_ref[...]
    lane = lax.broadcasted_iota(jnp.int32, (TQ, hw), 1)
    zero = jnp.zeros_like(q)
    qs_ref[0:TQ, :] = jnp.where(lane < HEAD_DIM, q, zero)
    qs_ref[TQ:2 * TQ, :] = jnp.where(lane >= HEAD_DIM, q, zero)
    m_ref[...] = jnp.full(m_ref.shape, -jnp.inf, F32)
    acc_ref[...] = jnp.zeros(acc_ref.shape, F32)

    def step(kc, masked):
        k0 = pl.multiple_of(kc * TK, TK)
        kb = k_ref[pl.ds(k0, TK), :]
        vb = vx_ref[pl.ds(k0, TK), :]
        for g in range(2 * TQ // RG):
            r0 = g * RG
            s = lax.dot_general(qs_ref[r0:r0 + RG, :], kb, (((1,), (1,)), ((), ())),
                                preferred_element_type=F32)
            if masked:
                row = lax.broadcasted_iota(jnp.int32, (RG, TK), 0) + (r0 % TQ)
                col = lax.broadcasted_iota(jnp.int32, (RG, TK), 1)
                s = jnp.where(col // CHUNK <= row // CHUNK, s, -jnp.inf)
            m_old = m_ref[r0:r0 + RG, :]
            m_new = jnp.maximum(m_old, jnp.max(s, axis=-1, keepdims=True))
            alpha = jnp.exp(m_old - m_new)
            p = jnp.exp(s - jnp.tile(m_new, (1, TK // LANES)))
            pv = jnp.dot(p.astype(BF16), vb, preferred_element_type=F32)
            acc_ref[r0:r0 + RG, :] = jnp.tile(alpha, (1, 2)) * acc_ref[r0:r0 + RG, :] + pv
            m_ref[r0:r0 + RG, :] = m_new

    def body(j, carry):
        for u in range(KV_UNROLL):
            step(KV_UNROLL * j + u, False)
        return carry

    def tail(kc, carry):
        step(kc, False)
        return carry

    n_full = qi // KV_UNROLL
    lax.fori_loop(0, n_full, body, 0)
    lax.fori_loop(n_full * KV_UNROLL, qi, tail, 0)
    step(qi, True)

    lam_v = lam_ref[...]
    lam = (jnp.exp(jnp.sum(lam_v[0:1, :] * lam_v[1:2, :], axis=-1, keepdims=True))
           - jnp.exp(jnp.sum(lam_v[2:3, :] * lam_v[3:4, :], axis=-1, keepdims=True)) + LAM_INIT)
    o0 = acc_ref[0:TQ, 0:hw] / acc_ref[0:TQ, hw:2 * hw]
    o1 = acc_ref[TQ:2 * TQ, 0:hw] / acc_ref[TQ:2 * TQ, hw:2 * hw]
    o = o0 - lam * o1
    ms = jnp.mean(o * o, axis=-1, keepdims=True)
    o = o * lax.rsqrt(ms + NORM_EPS) * sg_ref[...] * (1.0 - LAM_INIT)
    o_ref[...] = o.astype(BF16)


def _attention(proj, B, S, lam_vecs, subln_g):
    T = B * S
    nq = S // TQ
    hw = 2 * HEAD_DIM
    qc, kc, vc = COL_Q // hw, COL_K // hw, COL_V // hw
    return pl.pallas_call(
        _attn_kernel,
        grid=(B, N_HEADS, nq),
        in_specs=[
            pl.BlockSpec((TQ, hw), lambda b, h, i: (b * nq + i, qc + h)),
            pl.BlockSpec((S, hw), lambda b, h, i: (b, kc + h)),
            pl.BlockSpec((S, hw), lambda b, h, i: (b, vc + h)),
            pl.BlockSpec((4, HEAD_DIM), lambda b, h, i: (0, 0)),
            pl.BlockSpec((1, hw), lambda b, h, i: (0, 0)),
        ],
        out_specs=pl.BlockSpec((TQ, hw), lambda b, h, i: (b * nq + i, h)),
        out_shape=jax.ShapeDtypeStruct((T, ATTN_DIM), BF16),
        scratch_shapes=[pltpu.VMEM((2 * TQ, hw), BF16),
                        pltpu.VMEM((S, 2 * hw), BF16),
                        pltpu.VMEM((2 * TQ, LANES), F32),
                        pltpu.VMEM((2 * TQ, 2 * hw), F32)],
        compiler_params=_cparams(("parallel", "parallel", "arbitrary")),
    )(proj, proj, proj, lam_vecs, subln_g)


def _mix_kernel(a_ref, gb0_ref, gb1_ref, gbb_ref, ma_ref, x_ref, wao_ref, wo_ref, gf_ref,
                rw_ref, rb_ref, h_ref, uf_ref, lg_ref):
    half = D_MODEL // 2
    yb = jnp.dot(a_ref[...], wao_ref[...], preferred_element_type=F32)
    g0 = jax.nn.sigmoid(gb0_ref[...].astype(F32) + gbb_ref[:, 0:half])
    g1 = jax.nn.sigmoid(gb1_ref[...].astype(F32) + gbb_ref[:, half:D_MODEL])
    m0 = ma_ref[:, 0:half].astype(F32) + g0 * yb[:, 0:half]
    m1 = ma_ref[:, half:D_MODEL].astype(F32) + g1 * yb[:, half:D_MODEL]
    hh = (jnp.dot(m0.astype(BF16), wo_ref[0:half, :], preferred_element_type=F32)
          + jnp.dot(m1.astype(BF16), wo_ref[half:D_MODEL, :], preferred_element_type=F32))
    h = x_ref[...] + hh
    h_ref[...] = h
    ms = jnp.mean(h * h, axis=-1, keepdims=True)
    u = h * lax.rsqrt(ms + NORM_EPS) * gf_ref[...]
    for s in range(ROW_TILES):
        uf_ref[pl.ds(s, TM_MIX, stride=ROW_TILES), :] = u[:, s * LANES:(s + 1) * LANES]
    lg_ref[...] = jnp.dot(u, rw_ref[...], preferred_element_type=F32,
                          precision=lax.Precision.HIGHEST) + rb_ref[...]


def _mix(attn_o, proj, gate_b_b, m_a, x2, wao_bf, wo_bf, g_ffn, router_w, router_b):
    T = x2.shape[0]
    gcol = (COL_G + D_MODEL) // CONV_DIM
    const = lambda i: (0, 0)
    row = lambda i: (i, 0)
    return pl.pallas_call(
        _mix_kernel,
        grid=(T // TM_MIX,),
        in_specs=[
            pl.BlockSpec((TM_MIX, ATTN_DIM), row),
            pl.BlockSpec((TM_MIX, CONV_DIM), lambda i: (i, gcol)),
            pl.BlockSpec((TM_MIX, CONV_DIM), lambda i: (i, gcol + 1)),
            pl.BlockSpec((1, D_MODEL), const),
            pl.BlockSpec((TM_MIX, D_MODEL), row),
            pl.BlockSpec((TM_MIX, D_MODEL), row),
            pl.BlockSpec((ATTN_DIM, D_MODEL), const),
            pl.BlockSpec((D_MODEL, D_MODEL), const),
            pl.BlockSpec((1, D_MODEL), const),
            pl.BlockSpec((D_MODEL, N_EXPERTS), const),
            pl.BlockSpec((1, N_EXPERTS), const),
        ],
        out_specs=[
            pl.BlockSpec((TM_MIX, D_MODEL), row),
            pl.BlockSpec((TM_MIX * ROW_TILES, LANES), row),
            pl.BlockSpec((TM_MIX, N_EXPERTS), row),
        ],
        out_shape=[
            jax.ShapeDtypeStruct((T, D_MODEL), F32),
            jax.ShapeDtypeStruct((T * ROW_TILES, LANES), F32),
            jax.ShapeDtypeStruct((T, N_EXPERTS), F32),
        ],
        compiler_params=_cparams(("parallel",)),
    )(attn_o, proj, proj, gate_b_b, m_a, x2, wao_bf, wo_bf, g_ffn, router_w, router_b)


def _route_kernel(lg_ref, idx_ref, w_ref, rank_ref, cnt_ref, carry_ref):
    i = pl.program_id(0)

    @pl.when(i == 0)
    def _():
        carry_ref[...] = jnp.zeros(carry_ref.shape, F32)

    vals = lg_ref[...]
    lane = lax.broadcasted_iota(jnp.int32, (TR, N_EXPERTS), 1)
    sel = jnp.zeros((TR, N_EXPERTS), F32)
    tops, idxs, hots = [], [], []
    for _ in range(TOP_K):
        m = jnp.max(vals, axis=-1, keepdims=True)
        ix = jnp.min(jnp.where(vals == m, lane, N_EXPERTS), axis=-1, keepdims=True)
        hot = lane == ix
        vals = jnp.where(hot, -jnp.inf, vals)
        sel = sel + hot.astype(F32)
        tops.append(m)
        idxs.append(ix)
        hots.append(hot)
    es = [jnp.exp(t - tops[0]) for t in tops]
    den = es[0] + es[1] + es[2] + es[3]

    r_i = lax.broadcasted_iota(jnp.int32, (TR, TR), 0)
    c_i = lax.broadcasted_iota(jnp.int32, (TR, TR), 1)
    tri = jnp.where(c_i < r_i, 1.0, 0.0).astype(BF16)
    before = jnp.dot(tri, sel.astype(BF16), preferred_element_type=F32) + carry_ref[...]
    carry_ref[...] = carry_ref[...] + jnp.sum(sel, axis=0, keepdims=True)
    cnt_ref[...] = carry_ref[...]

    lane4 = lax.broadcasted_iota(jnp.int32, (TR, TOP_K), 1)
    idx_o = jnp.zeros((TR, TOP_K), jnp.int32)
    w_o = jnp.zeros((TR, TOP_K), F32)
    rk_o = jnp.zeros((TR, TOP_K), F32)
    for k in range(TOP_K):
        rk = jnp.sum(jnp.where(hots[k], before, 0.0), axis=-1, keepdims=True)
        idx_o = jnp.where(lane4 == k, idxs[k], idx_o)
        w_o = jnp.where(lane4 == k, es[k] / den, w_o)
        rk_o = jnp.where(lane4 == k, rk, rk_o)
    idx_ref[...] = idx_o
    w_ref[...] = w_o
    rank_ref[...] = rk_o.astype(jnp.int32)


def _route(logits):
    T = logits.shape[0]
    row = lambda i: (i, 0)
    return pl.pallas_call(
        _route_kernel,
        grid=(T // TR,),
        in_specs=[pl.BlockSpec((TR, N_EXPERTS), row)],
        out_specs=[
            pl.BlockSpec((TR, TOP_K), row),
            pl.BlockSpec((TR, TOP_K), row),
            pl.BlockSpec((TR, TOP_K), row),
            pl.BlockSpec((1, N_EXPERTS), lambda i: (0, 0)),
        ],
        out_shape=[
            jax.ShapeDtypeStruct((T, TOP_K), jnp.int32),
            jax.ShapeDtypeStruct((T, TOP_K), F32),
            jax.ShapeDtypeStruct((T, TOP_K), jnp.int32),
            jax.ShapeDtypeStruct((1, N_EXPERTS), F32),
        ],
        scratch_shapes=[pltpu.VMEM((1, N_EXPERTS), F32)],
        compiler_params=_cparams(("arbitrary",)),
    )(logits)


def _dispatch_kernel(dest_ref, u_ref, xs_in_hbm, xs_hbm, sem):
    del xs_in_hbm
    def copy(t, k):
        src = pl.multiple_of(t * ROW_TILES, ROW_TILES)
        dst = pl.multiple_of(dest_ref[t * TOP_K + k] * ROW_TILES, ROW_TILES)
        return pltpu.make_async_copy(u_ref.at[pl.ds(src, ROW_TILES)], xs_hbm.at[pl.ds(dst, ROW_TILES)], sem)

    def issue(t, carry):
        for k in range(TOP_K):
            copy(t, k).start()
        return carry

    def drain(t, carry):
        for k in range(TOP_K):
            copy(t, k).wait()
        return carry

    lax.fori_loop(0, TB, issue, 0, unroll=4)
    lax.fori_loop(0, TB, drain, 0, unroll=4)


def _dispatch(dest_flat, uf, n_rows):
    T = uf.shape[0] // ROW_TILES
    xs0 = jnp.zeros((n_rows * ROW_TILES, LANES), F32)
    return pl.pallas_call(
        _dispatch_kernel,
        grid=(T // TB,),
        in_specs=[
            pl.BlockSpec((TB * TOP_K,), lambda i: (i,), memory_space=pltpu.SMEM),
            pl.BlockSpec((TB * ROW_TILES, LANES), lambda i: (i, 0)),
            pl.BlockSpec(memory_space=pl.ANY),
        ],
        out_specs=pl.BlockSpec(memory_space=pl.ANY),
        out_shape=jax.ShapeDtypeStruct((n_rows * ROW_TILES, LANES), F32),
        scratch_shapes=[pltpu.SemaphoreType.DMA],
        input_output_aliases={2: 0},
        compiler_params=_cparams(("arbitrary",)),
    )(dest_flat, uf, xs0)


def _split_gu_kernel(w_ref, wg_ref, wu_ref):
    r_i = lax.broadcasted_iota(jnp.int32, (PERM_W, PERM_W), 0)
    c_i = lax.broadcasted_iota(jnp.int32, (PERM_W, PERM_W), 1)
    src = jnp.where(c_i < PERM_W // 2, 2 * c_i, 2 * (c_i - PERM_W // 2) + 1)
    perm = jnp.where(r_i == src, 1.0, 0.0).astype(BF16)
    hw = PERM_W // 2
    for c in range(TC_PREP // PERM_W):
        chunk = w_ref[:, c * PERM_W:(c + 1) * PERM_W].astype(BF16)
        res = jnp.dot(chunk, perm, preferred_element_type=F32).astype(BF16)
        wg_ref[:, c * hw:(c + 1) * hw] = res[:, 0:hw]
        wu_ref[:, c * hw:(c + 1) * hw] = res[:, hw:PERM_W]


def _split_gu(w_gu2):
    rows, cols = w_gu2.shape
    out = jax.ShapeDtypeStruct((rows, cols // 2), BF16)
    return pl.pallas_call(
        _split_gu_kernel,
        grid=(rows // TR_PREP, cols // TC_PREP),
        in_specs=[pl.BlockSpec((TR_PREP, TC_PREP), lambda i, j: (i, j))],
        out_specs=[pl.BlockSpec((TR_PREP, TC_PREP // 2), lambda i, j: (i, j)),
                   pl.BlockSpec((TR_PREP, TC_PREP // 2), lambda i, j: (i, j))],
        out_shape=[out, out],
        compiler_params=_cparams(("parallel", "parallel")),
    )(w_gu2)


def _ffn_gu_kernel(be_ref, nv_ref, xs_ref, wg_ref, wu_ref, bg_ref, bu_ref, act_ref):
    b = pl.program_id(1)

    @pl.when(b < nv_ref[0])
    def _():
        x = jnp.concatenate(
            [xs_ref[pl.ds(s, M_BLK, stride=ROW_TILES), :].astype(BF16) for s in range(ROW_TILES)], axis=-1)
        gate = jnp.dot(x, wg_ref[0], preferred_element_type=F32) + bg_ref[0]
        up = jnp.dot(x, wu_ref[0], preferred_element_type=F32) + bu_ref[0]
        gate = jnp.minimum(gate, SWIGLU_LIMIT)
        up = jnp.clip(up, -SWIGLU_LIMIT, SWIGLU_LIMIT)
        act = (up + 1.0) * gate * jax.nn.sigmoid(SWIGLU_ALPHA * gate)
        act_ref[...] = act.astype(BF16)

    @pl.when(b >= nv_ref[0])
    def _():
        act_ref[...] = jnp.zeros(act_ref.shape, BF16)


def _ffn_down_kernel(be_ref, nv_ref, act_ref, wd_ref, bd_ref, y_ref):
    b = pl.program_id(0)

    @pl.when(b < nv_ref[0])
    def _():
        y = jnp.dot(act_ref[...], wd_ref[0], preferred_element_type=F32) + bd_ref[0]
        for s in range(ROW_TILES):
            y_ref[pl.ds(s, M_BLK, stride=ROW_TILES), :] = y[:, s * LANES:(s + 1) * LANES]

    @pl.when(b >= nv_ref[0])
    def _():
        y_ref[...] = jnp.zeros(y_ref.shape, F32)


def _blk(b, nv):
    return jnp.minimum(b, nv[0] - 1)


def _ffn_gu(block_e, nvalid, xs, wg, wu, bg, bu):
    n_rows = xs.shape[0] // ROW_TILES
    nb = n_rows // M_BLK
    grid_spec = pltpu.PrefetchScalarGridSpec(
        num_scalar_prefetch=2,
        grid=(D_FF // TF, nb),
        in_specs=[
            pl.BlockSpec((M_BLK * ROW_TILES, LANES), lambda n, b, be, nv: (_blk(b, nv), 0)),
            pl.BlockSpec((1, D_MODEL, TF), lambda n, b, be, nv: (be[_blk(b, nv)], 0, n)),
            pl.BlockSpec((1, D_MODEL, TF), lambda n, b, be, nv: (be[_blk(b, nv)], 0, n)),
            pl.BlockSpec((1, 1, TF), lambda n, b, be, nv: (be[_blk(b, nv)], 0, n)),
            pl.BlockSpec((1, 1, TF), lambda n, b, be, nv: (be[_blk(b, nv)], 0, n)),
        ],
        out_specs=pl.BlockSpec((M_BLK, TF), lambda n, b, be, nv: (b, n)),
    )
    return pl.pallas_call(
        _ffn_gu_kernel,
        grid_spec=grid_spec,
        out_shape=jax.ShapeDtypeStruct((n_rows, D_FF), BF16),
        compiler_params=_cparams(("arbitrary", "arbitrary")),
    )(block_e, nvalid, xs, wg, wu, bg, bu)


def _ffn_down(block_e, nvalid, act, wd, bd):
    n_rows = act.shape[0]
    nb = n_rows // M_BLK
    grid_spec = pltpu.PrefetchScalarGridSpec(
        num_scalar_prefetch=2,
        grid=(nb,),
        in_specs=[
            pl.BlockSpec((M_BLK, D_FF), lambda b, be, nv: (_blk(b, nv), 0)),
            pl.BlockSpec((1, D_FF, D_MODEL), lambda b, be, nv: (be[_blk(b, nv)], 0, 0)),
            pl.BlockSpec((1, 1, D_MODEL), lambda b, be, nv: (be[_blk(b, nv)], 0, 0)),
        ],
        out_specs=pl.BlockSpec((M_BLK * ROW_TILES, LANES), lambda b, be, nv: (b, 0)),
    )
    return pl.pallas_call(
        _ffn_down_kernel,
        grid_spec=grid_spec,
        out_shape=jax.ShapeDtypeStruct((n_rows * ROW_TILES, LANES), F32),
        compiler_params=_cparams(("arbitrary",)),
    )(block_e, nvalid, act, wd, bd)


def _combine_kernel(dest_ref, ys_hbm, h_ref, w_ref, gfin_ref, o_ref, buf, hbuf, sems):
    rows_c = TOP_K * CT_COMB

    def copy(c, j):
        r = c * rows_c + j
        src = pl.multiple_of(dest_ref[r] * ROW_TILES, ROW_TILES)
        dst = pl.multiple_of(r * ROW_TILES, ROW_TILES)
        return pltpu.make_async_copy(ys_hbm.at[pl.ds(src, ROW_TILES)], buf.at[pl.ds(dst, ROW_TILES)],
                                     sems.at[c])

    for c in range(TB // CT_COMB):
        def issue(j, carry, c=c):
            copy(c, j).start()
            return carry
        lax.fori_loop(0, rows_c, issue, 0, unroll=8)

    for c in range(TB // CT_COMB):
        def drain(j, carry, c=c):
            copy(c, j).wait()
            return carry
        lax.fori_loop(0, rows_c, drain, 0, unroll=8)

        t0 = c * CT_COMB
        w = w_ref[t0:t0 + CT_COMB, :]
        for s in range(ROW_TILES):
            hs = h_ref[t0:t0 + CT_COMB, s * LANES:(s + 1) * LANES]
            for k in range(TOP_K):
                first = (c * rows_c + k * CT_COMB) * ROW_TILES + s
                hs = hs + w[:, k:k + 1] * buf[pl.ds(first, CT_COMB, stride=ROW_TILES), :]
            hbuf[t0:t0 + CT_COMB, s * LANES:(s + 1) * LANES] = hs
        h = hbuf[t0:t0 + CT_COMB, :]
        ms = jnp.mean(h * h, axis=-1, keepdims=True)
        o_ref[t0:t0 + CT_COMB, :] = h * lax.rsqrt(ms + NORM_EPS) * gfin_ref[...]


def _combine(dest_kmajor, ys, h1, wts, g_final):
    T = h1.shape[0]
    row = lambda i: (i, 0)
    return pl.pallas_call(
        _combine_kernel,
        grid=(T // TB,),
        in_specs=[
            pl.BlockSpec((TB * TOP_K,), lambda i: (i,), memory_space=pltpu.SMEM),
            pl.BlockSpec(memory_space=pl.ANY),
            pl.BlockSpec((TB, D_MODEL), row),
            pl.BlockSpec((TB, TOP_K), row),
            pl.BlockSpec((1, D_MODEL), lambda i: (0, 0)),
        ],
        out_specs=pl.BlockSpec((TB, D_MODEL), row),
        out_shape=jax.ShapeDtypeStruct((T, D_MODEL), F32),
        scratch_shapes=[pltpu.VMEM((TB * TOP_K * ROW_TILES, LANES), F32),
                        pltpu.VMEM((TB, D_MODEL), F32), pltpu.SemaphoreType.DMA((TB // CT_COMB,))],
        compiler_params=_cparams(("arbitrary",)),
    )(dest_kmajor, ys, h1, wts, g_final)


def _rope_tables(positions):
    half = ROT_DIM // 2
    inv = ROPE_THETA ** (-jnp.arange(0, ROT_DIM, 2, dtype=F32) / ROT_DIM)
    ang = positions.astype(F32).reshape(-1, 1) * inv
    cos, sin = jnp.cos(ang), jnp.sin(ang)
    T = ang.shape[0]
    rest = HEAD_DIM - ROT_DIM
    c64 = jnp.concatenate([cos, cos, jnp.ones((T, rest), F32)], axis=-1)
    s1_64 = jnp.concatenate([-sin, jnp.zeros((T, HEAD_DIM - half), F32)], axis=-1)
    s2_64 = jnp.concatenate([jnp.zeros((T, half), F32), sin, jnp.zeros((T, rest), F32)], axis=-1)
    dup = lambda t: jnp.concatenate([t, t], axis=-1)
    return dup(c64), dup(s1_64), dup(s2_64)


def kernel(x, positions, norm_mix_g, w_in, gate_b, conv_dw_w, conv_dw_b, conv_ln_g, conv_ln_b, w_conv_out, lambda_q1, lambda_k1, lambda_q2, lambda_k2, attn_subln_g, w_attn_out, w_out, norm_ffn_g, router_w, router_b, w_gu, b_gu, w_down, b_down, norm_final_g):
    B, S, D = x.shape
    T = B * S
    x2 = x.reshape(T, D)
    ctab, s1tab, s2tab = _rope_tables(positions)

    proj = _inproj(x2, norm_mix_g[0].reshape(1, D), w_in[0].astype(BF16), ctab, s1tab, s2tab)
    m_a = _conv_branch(proj, B, S, conv_dw_w[0], conv_dw_b[0].reshape(1, -1),
                       conv_ln_g[0].reshape(1, -1), conv_ln_b[0].reshape(1, -1),
                       w_conv_out[0].astype(BF16), gate_b[0, :D].reshape(1, D))
    lam_vecs = jnp.stack([lambda_q1[0], lambda_k1[0], lambda_q2[0], lambda_k2[0]])
    attn_o = _attention(proj, B, S, lam_vecs, attn_subln_g[0].reshape(1, -1))
    h1, uf, logits = _mix(attn_o, proj, gate_b[0, D:].reshape(1, D), m_a, x2,
                          w_attn_out[0].astype(BF16), w_out[0].astype(BF16),
                          norm_ffn_g[0].reshape(1, D), router_w[0], router_b[0].reshape(1, -1))

    idx, wts, rank, counts = _route(logits)
    counts = counts[0].astype(jnp.int32)
    padded = (counts + M_BLK - 1) // M_BLK * M_BLK
    pad_end = jnp.cumsum(padded)
    pad_start = pad_end - padded
    dest = pad_start[idx] + rank
    n_blocks = (T * TOP_K) // M_BLK + N_EXPERTS
    n_rows = n_blocks * M_BLK
    blk_start = jnp.arange(n_blocks, dtype=jnp.int32) * M_BLK
    block_e = jnp.minimum(jnp.sum((pad_end[None, :] <= blk_start[:, None]).astype(jnp.int32), axis=1),
                          N_EXPERTS - 1)
    nvalid = (pad_end[-1:] // M_BLK).astype(jnp.int32)

    xs = _dispatch(dest.reshape(-1), uf, n_rows)
    wg, wu = _split_gu(w_gu[0].reshape(N_EXPERTS * D, 2 * D_FF))
    wg = wg.reshape(N_EXPERTS, D, D_FF)
    wu = wu.reshape(N_EXPERTS, D, D_FF)
    bg = b_gu[0, :, 0::2].reshape(N_EXPERTS, 1, D_FF)
    bu = b_gu[0, :, 1::2].reshape(N_EXPERTS, 1, D_FF)
    act = _ffn_gu(block_e, nvalid, xs, wg, wu, bg, bu)
    ys = _ffn_down(block_e, nvalid, act, w_down[0].astype(BF16), b_down[0].reshape(N_EXPERTS, 1, D))

    dest_km = dest.reshape(T // CT_COMB, CT_COMB, TOP_K).transpose(0, 2, 1).reshape(-1)
    out = _combine(dest_km, ys, h1, wts, norm_final_g.reshape(1, D))
    return out.reshape(B, S, D)
```

```python
import functools

import jax
import jax.numpy as jnp
from jax import lax
from jax.experimental import pallas as pl
from jax.experimental.pallas import tpu as pltpu

F32 = jnp.float32
BF16 = jnp.bfloat16

D_MODEL = 2048
CHUNK = 64
CONV_DIM = 1024
CONV_WIDTH = 31
N_HEADS = 8
HEAD_DIM = 64
ATTN_DIM = N_HEADS * 2 * HEAD_DIM
ROT_DIM = HEAD_DIM // 4
ROPE_THETA = 500000.0
N_BRANCH = 2
IN_COLS = 2 * CONV_DIM + 3 * ATTN_DIM + N_BRANCH * D_MODEL
N_EXPERTS = 32
TOP_K = 4
D_FF = 2048
SWIGLU_LIMIT = 7.0
SWIGLU_ALPHA = 1.702
NORM_EPS = 1e-6
LN_EPS = 1e-5
LAM_INIT = 0.2

LANES = 128
SUBLANES = 8
VMEM_LIMIT = 56 * 1024 * 1024

COL_Q = 2 * CONV_DIM
COL_K = COL_Q + ATTN_DIM
COL_V = COL_K + ATTN_DIM
COL_G = COL_V + ATTN_DIM

TM_IN = 1024
TN_IN = 1024
TS_CONV = 256
HALO = 32
TQ = 512
TK = 512
RG = 128
KV_UNROLL = 4
TM_MIX = 512
RG_MIX = 128
TR = 512
TB = 256
CT_COMB = 64
M_BLK = 512
TF = 512
ROW_TILES = D_MODEL // LANES
PERM_W = 256


def _cparams(sem):
    return pltpu.CompilerParams(dimension_semantics=sem, vmem_limit_bytes=VMEM_LIMIT)


def _inproj_kernel(x_ref, g_ref, w_ref, c_ref, s1_ref, s2_ref, o_ref, u_ref):
    j = pl.program_id(1)

    @pl.when(j == 0)
    def _():
        x = x_ref[...]
        ms = jnp.mean(x * x, axis=-1, keepdims=True)
        u_ref[...] = (x * lax.rsqrt(ms + NORM_EPS) * g_ref[...]).astype(BF16)

    acc = jnp.dot(u_ref[...], w_ref[...], preferred_element_type=F32)
    q_tile = COL_Q // TN_IN
    k_tile = COL_K // TN_IN
    is_rope = jnp.logical_or(j == q_tile, j == k_tile)

    @pl.when(is_rope)
    def _():
        scale = jnp.where(j == q_tile, HEAD_DIM ** -0.5, 1.0).astype(F32)
        c = c_ref[...]
        s1 = s1_ref[...]
        s2 = s2_ref[...]
        for hh in range(TN_IN // LANES):
            t = acc[:, hh * LANES:(hh + 1) * LANES]
            r = (t * c + pltpu.roll(t, LANES - ROT_DIM // 2, 1) * s1
                 + pltpu.roll(t, ROT_DIM // 2, 1) * s2)
            o_ref[:, hh * LANES:(hh + 1) * LANES] = (r * scale).astype(BF16)

    @pl.when(jnp.logical_not(is_rope))
    def _():
        o_ref[...] = acc.astype(BF16)


def _inproj(x2, g, w_bf, ctab, s1tab, s2tab):
    T = x2.shape[0]
    return pl.pallas_call(
        _inproj_kernel,
        grid=(T // TM_IN, IN_COLS // TN_IN),
        in_specs=[
            pl.BlockSpec((TM_IN, D_MODEL), lambda i, j: (i, 0)),
            pl.BlockSpec((1, D_MODEL), lambda i, j: (0, 0)),
            pl.BlockSpec((D_MODEL, TN_IN), lambda i, j: (0, j)),
            pl.BlockSpec((TM_IN, LANES), lambda i, j: (i, 0)),
            pl.BlockSpec((TM_IN, LANES), lambda i, j: (i, 0)),
            pl.BlockSpec((TM_IN, LANES), lambda i, j: (i, 0)),
        ],
        out_specs=pl.BlockSpec((TM_IN, TN_IN), lambda i, j: (i, j)),
        out_shape=jax.ShapeDtypeStruct((T, IN_COLS), BF16),
        scratch_shapes=[pltpu.VMEM((TM_IN, D_MODEL), BF16)],
        compiler_params=_cparams(("parallel", "arbitrary")),
    )(x2, g, w_bf, ctab, s1tab, s2tab)


def _conv_kernel(val_ref, gate_ref, hval_ref, hgate_ref, dww_ref, dwb_ref, lng_ref, lnb_ref,
                 wco_ref, ga0_ref, ga1_ref, gab_ref, o_ref, hbuf, cbuf, shbuf):
    i = pl.program_id(1)
    ts = TS_CONV
    halo = hval_ref[...].astype(F32) * jax.nn.sigmoid(hgate_ref[...].astype(F32))
    hbuf[0:HALO, :] = jnp.where(i == 0, 0.0, halo)
    hbuf[HALO:HALO + ts, :] = val_ref[...].astype(F32) * jax.nn.sigmoid(gate_ref[...].astype(F32))

    base = HALO - (CONV_WIDTH - 1)

    def lane_chunk(c, carry):
        l0 = pl.multiple_of(c * LANES, LANES)
        acc = jnp.zeros((ts, LANES), F32) + dwb_ref[:, pl.ds(l0, LANES)]
        for r in range(SUBLANES):
            taps = [j for j in range(CONV_WIDTH) if (base + j) % SUBLANES == r]
            q_max = max((base + j) // SUBLANES for j in taps)
            n = SUBLANES * q_max + ts
            shbuf[0:n, :] = hbuf[r:r + n, pl.ds(l0, LANES)]
            for j in taps:
                q = (base + j) // SUBLANES
                acc = acc + dww_ref[j:j + 1, pl.ds(l0, LANES)] * shbuf[SUBLANES * q:SUBLANES * q + ts, :]
        cbuf[:, pl.ds(l0, LANES)] = acc
        return carry

    lax.fori_loop(0, CONV_DIM // LANES, lane_chunk, 0)

    y = cbuf[...]
    mu = jnp.mean(y, axis=-1, keepdims=True)
    yc = y - mu
    var = jnp.mean(yc * yc, axis=-1, keepdims=True)
    yn = yc * lax.rsqrt(var + LN_EPS) * lng_ref[...] + lnb_ref[...]
    a = yn * jax.nn.sigmoid(yn)
    out = jnp.dot(a.astype(BF16), wco_ref[...], preferred_element_type=F32)
    half = D_MODEL // 2
    g0 = jax.nn.sigmoid(ga0_ref[...].astype(F32) + gab_ref[:, 0:half])
    g1 = jax.nn.sigmoid(ga1_ref[...].astype(F32) + gab_ref[:, half:D_MODEL])
    o_ref[:, 0:half] = (out[:, 0:half] * g0).astype(BF16)
    o_ref[:, half:D_MODEL] = (out[:, half:D_MODEL] * g1).astype(BF16)


def _conv_branch(proj, B, S, dww, dwb, lng, lnb, wco_bf, gate_b_a):
    T = B * S
    nts = S // TS_CONV
    hpt = TS_CONV // HALO
    gcol = COL_G // CONV_DIM

    def halo_map(col):
        return lambda b, i: (jnp.maximum((b * nts + i) * hpt - 1, 0), col)

    const = lambda b, i: (0, 0)
    return pl.pallas_call(
        _conv_kernel,
        grid=(B, nts),
        in_specs=[
            pl.BlockSpec((TS_CONV, CONV_DIM), lambda b, i: (b * nts + i, 0)),
            pl.BlockSpec((TS_CONV, CONV_DIM), lambda b, i: (b * nts + i, 1)),
            pl.BlockSpec((HALO, CONV_DIM), halo_map(0)),
            pl.BlockSpec((HALO, CONV_DIM), halo_map(1)),
            pl.BlockSpec((CONV_WIDTH, CONV_DIM), const),
            pl.BlockSpec((1, CONV_DIM), const),
            pl.BlockSpec((1, CONV_DIM), const),
            pl.BlockSpec((1, CONV_DIM), const),
            pl.BlockSpec((CONV_DIM, D_MODEL), const),
            pl.BlockSpec((TS_CONV, CONV_DIM), lambda b, i: (b * nts + i, gcol)),
            pl.BlockSpec((TS_CONV, CONV_DIM), lambda b, i: (b * nts + i, gcol + 1)),
            pl.BlockSpec((1, D_MODEL), const),
        ],
        out_specs=pl.BlockSpec((TS_CONV, D_MODEL), lambda b, i: (b * nts + i, 0)),
        out_shape=jax.ShapeDtypeStruct((T, D_MODEL), BF16),
        scratch_shapes=[pltpu.VMEM((HALO + TS_CONV, CONV_DIM), F32),
                        pltpu.VMEM((TS_CONV, CONV_DIM), F32),
                        pltpu.VMEM((HALO + TS_CONV, LANES), F32)],
        compiler_params=_cparams(("parallel", "arbitrary")),
    )(proj, proj, proj, proj, dww, dwb, lng, lnb, wco_bf, proj, proj, gate_b_a)


def _attn_kernel(q_ref, k_ref, v_ref, lam_ref, sg_ref, o_ref, qs_ref, vx_ref, m_ref, acc_ref):
    qi = pl.program_id(2)
    hw = 2 * HEAD_DIM

    @pl.when(qi == 0)
    def _():
        vx_ref[:, 0:hw] = v_ref[...]
        vx_ref[:, hw:2 * hw] = jnp.ones((v_ref.shape[0], hw), BF16)

    q = q_ref[...]
    lane = lax.broadcasted_iota(jnp.int32, (TQ, hw), 1)
    zero = jnp.zeros_like(q)
    qs_ref[0:TQ, :] = jnp.where(lane < HEAD_DIM, q, zero)
    qs_ref[TQ:2 * TQ, :] = jnp.where(lane >= HEAD_DIM, q, zero)
    m_ref[...] = jnp.full(m_ref.shape, -jnp.inf, F32)
    acc_ref[...] = jnp.zeros(acc_ref.shape, F32)

    def step(kc, masked):
        k0 = pl.multiple_of(kc * TK, TK)
        kb = k_ref[pl.ds(k0, TK), :]
        vb = vx_ref[pl.ds(k0, TK), :]
        for g in range(2 * TQ // RG):
            r0 = g * RG
            s = lax.dot_general(qs_ref[r0:r0 + RG, :], kb, (((1,), (1,)), ((), ())),
                                preferred_element_type=F32)
            if masked:
                row = lax.broadcasted_iota(jnp.int32, (RG, TK), 0) + (r0 % TQ)
                col = lax.broadcasted_iota(jnp.int32, (RG, TK), 1)
                s = jnp.where(col // CHUNK <= row // CHUNK, s, -jnp.inf)
            m_old = m_ref[r0:r0 + RG, :]
            m_new = jnp.maximum(m_old, jnp.max(s, axis=-1, keepdims=True))
            alpha = jnp.exp(m_old - m_new)
            p = jnp.exp(s - jnp.tile(m_new, (1, TK // LANES)))
            pv = jnp.dot(p.astype(BF16), vb, preferred_element_type=F32)
            acc_ref[r0:r0 + RG, :] = jnp.tile(alpha, (1, 2)) * acc_ref[r0:r0 + RG, :] + pv
            m_ref[r0:r0 + RG, :] = m_new

    def body(j, carry):
        for u in range(KV_UNROLL):
            step(KV_UNROLL * j + u, False)
        return carry

    def tail(kc, carry):
        step(kc, False)
        return carry

    n_full = qi // KV_UNROLL
    lax.fori_loop(0, n_full, body, 0)
    lax.fori_loop(n_full * KV_UNROLL, qi, tail, 0)
    step(qi, True)

    lam_v = lam_ref[...]
    lam = (jnp.exp(jnp.sum(lam_v[0:1, :] * lam_v[1:2, :], axis=-1, keepdims=True))
           - jnp.exp(jnp.sum(lam_v[2:3, :] * lam_v[3:4, :], axis=-1, keepdims=True)) + LAM_INIT)
    o0 = acc_ref[0:TQ, 0:hw] / acc_ref[0:TQ, hw:2 * hw]
    o1 = acc_ref[TQ:2 * TQ, 0:hw] / acc_ref[TQ:2 * TQ, hw:2 * hw]
    o = o0 - lam * o1
    ms = jnp.mean(o * o, axis=-1, keepdims=True)
    o = o * lax.rsqrt(ms + NORM_EPS) * sg_ref[...] * (1.0 - LAM_INIT)
    o_ref[...] = o.astype(BF16)


def _attention(proj, B, S, lam_vecs, subln_g):
    T = B * S
    nq = S // TQ
    hw = 2 * HEAD_DIM
    qc, kc, vc = COL_Q // hw, COL_K // hw, COL_V // hw
    return pl.pallas_call(
        _attn_kernel,
        grid=(B, N_HEADS, nq),
        in_specs=[
            pl.BlockSpec((TQ, hw), lambda b, h, i: (b * nq + i, qc + h)),
            pl.BlockSpec((S, hw), lambda b, h, i: (b, kc + h)),
            pl.BlockSpec((S, hw), lambda b, h, i: (b, vc + h)),
            pl.BlockSpec((4, HEAD_DIM), lambda b, h, i: (0, 0)),
            pl.BlockSpec((1, hw), lambda b, h, i: (0, 0)),
        ],
        out_specs=pl.BlockSpec((TQ, hw), lambda b, h, i: (b * nq + i, h)),
        out_shape=jax.ShapeDtypeStruct((T, ATTN_DIM), BF16),
        scratch_shapes=[pltpu.VMEM((2 * TQ, hw), BF16),
                        pltpu.VMEM((S, 2 * hw), BF16),
                        pltpu.VMEM((2 * TQ, LANES), F32),
                        pltpu.VMEM((2 * TQ, 2 * hw), F32)],
        compiler_params=_cparams(("parallel", "parallel", "arbitrary")),
    )(proj, proj, proj, lam_vecs, subln_g)


def _mix_kernel(a_ref, gb0_ref, gb1_ref, gbb_ref, ma_ref, x_ref, wao_ref, wo_ref, gf_ref,
                rwt_ref, rb_ref, h_ref, uf_ref, lg_ref):
    half = D_MODEL // 2
    for g in range(TM_MIX // RG_MIX):
        r0 = g * RG_MIX
        yb = jnp.dot(a_ref[r0:r0 + RG_MIX, :], wao_ref[...], preferred_element_type=F32)
        g0 = jax.nn.sigmoid(gb0_ref[r0:r0 + RG_MIX, :].astype(F32) + gbb_ref[:, 0:half])
        g1 = jax.nn.sigmoid(gb1_ref[r0:r0 + RG_MIX, :].astype(F32) + gbb_ref[:, half:D_MODEL])
        m0 = ma_ref[r0:r0 + RG_MIX, 0:half].astype(F32) + g0 * yb[:, 0:half]
        m1 = ma_ref[r0:r0 + RG_MIX, half:D_MODEL].astype(F32) + g1 * yb[:, half:D_MODEL]
        hh = (jnp.dot(m0.astype(BF16), wo_ref[0:half, :], preferred_element_type=F32)
              + jnp.dot(m1.astype(BF16), wo_ref[half:D_MODEL, :], preferred_element_type=F32))
        h = x_ref[r0:r0 + RG_MIX, :] + hh
        h_ref[r0:r0 + RG_MIX, :] = h
        ms = jnp.mean(h * h, axis=-1, keepdims=True)
        u = h * lax.rsqrt(ms + NORM_EPS) * gf_ref[...]
        for s in range(ROW_TILES):
            uf_ref[pl.ds(r0 * ROW_TILES + s, RG_MIX, stride=ROW_TILES), :] = u[:, s * LANES:(s + 1) * LANES]
        lg_ref[:, r0:r0 + RG_MIX] = lax.dot_general(
            rwt_ref[...], u, (((1,), (1,)), ((), ())), preferred_element_type=F32,
            precision=lax.Precision.HIGHEST) + rb_ref[...]


def _mix(attn_o, proj, gate_b_b, m_a, x2, wao_bf, wo_bf, g_ffn, router_w_t, router_b_col):
    T = x2.shape[0]
    gcol = (COL_G + D_MODEL) // CONV_DIM
    const = lambda i: (0, 0)
    row = lambda i: (i, 0)
    return pl.pallas_call(
        _mix_kernel,
        grid=(T // TM_MIX,),
        in_specs=[
            pl.BlockSpec((TM_MIX, ATTN_DIM), row),
            pl.BlockSpec((TM_MIX, CONV_DIM), lambda i: (i, gcol)),
            pl.BlockSpec((TM_MIX, CONV_DIM), lambda i: (i, gcol + 1)),
            pl.BlockSpec((1, D_MODEL), const),
            pl.BlockSpec((TM_MIX, D_MODEL), row),
            pl.BlockSpec((TM_MIX, D_MODEL), row),
            pl.BlockSpec((ATTN_DIM, D_MODEL), const, pipeline_mode=pl.Buffered(1)),
            pl.BlockSpec((D_MODEL, D_MODEL), const, pipeline_mode=pl.Buffered(1)),
            pl.BlockSpec((1, D_MODEL), const),
            pl.BlockSpec((N_EXPERTS, D_MODEL), const),
            pl.BlockSpec((N_EXPERTS, 1), const),
        ],
        out_specs=[
            pl.BlockSpec((TM_MIX, D_MODEL), row),
            pl.BlockSpec((TM_MIX * ROW_TILES, LANES), row),
            pl.BlockSpec((N_EXPERTS, TM_MIX), lambda i: (0, i)),
        ],
        out_shape=[
            jax.ShapeDtypeStruct((T, D_MODEL), F32),
            jax.ShapeDtypeStruct((T * ROW_TILES, LANES), F32),
            jax.ShapeDtypeStruct((N_EXPERTS, T), F32),
        ],
        compiler_params=_cparams(("parallel",)),
    )(attn_o, proj, proj, gate_b_b, m_a, x2, wao_bf, wo_bf, g_ffn, router_w_t, router_b_col)


def _route_kernel(lg_ref, idx_ref, w_ref, rank_ref, cnt_ref, carry_ref):
    i = pl.program_id(0)

    @pl.when(i == 0)
    def _():
        carry_ref[...] = jnp.zeros(carry_ref.shape, F32)

    vals = lg_ref[...]
    lane = lax.broadcasted_iota(jnp.int32, (TR, N_EXPERTS), 1)
    sel = jnp.zeros((TR, N_EXPERTS), F32)
    tops, idxs, hots = [], [], []
    for _ in range(TOP_K):
        m = jnp.max(vals, axis=-1, keepdims=True)
        ix = jnp.min(jnp.where(vals == m, lane, N_EXPERTS), axis=-1, keepdims=True)
        hot = lane == ix
        vals = jnp.where(hot, -jnp.inf, vals)
        sel = sel + hot.astype(F32)
        tops.append(m)
        idxs.append(ix)
        hots.append(hot)
    es = [jnp.exp(t - tops[0]) for t in tops]
    den = es[0] + es[1] + es[2] + es[3]

    r_i = lax.broadcasted_iota(jnp.int32, (TR, TR), 0)
    c_i = lax.broadcasted_iota(jnp.int32, (TR, TR), 1)
    tri = jnp.where(c_i < r_i, 1.0, 0.0).astype(BF16)
    before = jnp.dot(tri, sel.astype(BF16), preferred_element_type=F32) + carry_ref[...]
    carry_ref[...] = carry_ref[...] + jnp.sum(sel, axis=0, keepdims=True)
    cnt_ref[...] = carry_ref[...]

    lane4 = lax.broadcasted_iota(jnp.int32, (TR, TOP_K), 1)
    idx_o = jnp.zeros((TR, TOP_K), jnp.int32)
    w_o = jnp.zeros((TR, TOP_K), F32)
    rk_o = jnp.zeros((TR, TOP_K), F32)
    for k in range(TOP_K):
        rk = jnp.sum(jnp.where(hots[k], before, 0.0), axis=-1, keepdims=True)
        idx_o = jnp.where(lane4 == k, idxs[k], idx_o)
        w_o = jnp.where(lane4 == k, es[k] / den, w_o)
        rk_o = jnp.where(lane4 == k, rk, rk_o)
    idx_ref[...] = idx_o
    w_ref[...] = w_o
    rank_ref[...] = rk_o.astype(jnp.int32)


def _route(logits):
    T = logits.shape[0]
    row = lambda i: (i, 0)
    return pl.pallas_call(
        _route_kernel,
        grid=(T // TR,),
        in_specs=[pl.BlockSpec((TR, N_EXPERTS), row)],
        out_specs=[
            pl.BlockSpec((TR, TOP_K), row),
            pl.BlockSpec((TR, TOP_K), row),
            pl.BlockSpec((TR, TOP_K), row),
            pl.BlockSpec((1, N_EXPERTS), lambda i: (0, 0)),
        ],
        out_shape=[
            jax.ShapeDtypeStruct((T, TOP_K), jnp.int32),
            jax.ShapeDtypeStruct((T, TOP_K), F32),
            jax.ShapeDtypeStruct((T, TOP_K), jnp.int32),
            jax.ShapeDtypeStruct((1, N_EXPERTS), F32),
        ],
        scratch_shapes=[pltpu.VMEM((1, N_EXPERTS), F32)],
        compiler_params=_cparams(("arbitrary",)),
    )(logits)


def _dispatch_kernel(plo_ref, phi_ref, dest_ref, u_ref, xs_hbm, zbuf, sem, zsem):
    i = pl.program_id(0)

    @pl.when(i == 0)
    def _():
        zbuf[...] = jnp.zeros(zbuf.shape, F32)

    def copy(t, k):
        src = pl.multiple_of(t * ROW_TILES, ROW_TILES)
        dst = pl.multiple_of(dest_ref[t * TOP_K + k] * ROW_TILES, ROW_TILES)
        return pltpu.make_async_copy(u_ref.at[pl.ds(src, ROW_TILES)], xs_hbm.at[pl.ds(dst, ROW_TILES)], sem)

    def issue(t, carry):
        for k in range(TOP_K):
            copy(t, k).start()
        return carry

    def drain(t, carry):
        for k in range(TOP_K):
            copy(t, k).wait()
        return carry

    lax.fori_loop(0, TB, issue, 0, unroll=4)

    def zrow(r):
        dst = pl.multiple_of(r * ROW_TILES, ROW_TILES)
        return pltpu.make_async_copy(zbuf.at[pl.ds(0, ROW_TILES)], xs_hbm.at[pl.ds(dst, ROW_TILES)], zsem)

    def zblock(b):
        dst = pl.multiple_of(b * (M_BLK * ROW_TILES), M_BLK * ROW_TILES)
        return pltpu.make_async_copy(zbuf, xs_hbm.at[pl.ds(dst, M_BLK * ROW_TILES)], zsem)

    def span(fn):
        def go(r, carry):
            fn(r)
            return carry
        return go

    e = jnp.minimum(i, N_EXPERTS)
    lo = plo_ref[e]
    hi = phi_ref[e]
    n_row = jnp.where(i < N_EXPERTS, hi, lo)
    n_blk = jnp.where(i == N_EXPERTS, hi, lo)
    lax.fori_loop(lo, n_row, span(lambda r: zrow(r).start()), 0)
    lax.fori_loop(lo, n_blk, span(lambda b: zblock(b).start()), 0)
    lax.fori_loop(0, TB, drain, 0, unroll=4)
    lax.fori_loop(lo, n_row, span(lambda r: zrow(r).wait()), 0)
    lax.fori_loop(lo, n_blk, span(lambda b: zblock(b).wait()), 0)


def _dispatch(pad_lo, pad_hi, dest_flat, uf, n_rows):
    T = uf.shape[0] // ROW_TILES
    assert T // TB > N_EXPERTS
    grid_spec = pltpu.PrefetchScalarGridSpec(
        num_scalar_prefetch=2,
        grid=(T // TB,),
        in_specs=[
            pl.BlockSpec((TB * TOP_K,), lambda i, lo, hi: (i,), memory_space=pltpu.SMEM),
            pl.BlockSpec((TB * ROW_TILES, LANES), lambda i, lo, hi: (i, 0)),
        ],
        out_specs=pl.BlockSpec(memory_space=pl.ANY),
        scratch_shapes=[pltpu.VMEM((M_BLK * ROW_TILES, LANES), F32), pltpu.SemaphoreType.DMA,
                        pltpu.SemaphoreType.DMA],
    )
    return pl.pallas_call(
        _dispatch_kernel,
        grid_spec=grid_spec,
        out_shape=jax.ShapeDtypeStruct((n_rows * ROW_TILES, LANES), F32),
        compiler_params=_cparams(("arbitrary",)),
    )(pad_lo, pad_hi, dest_flat, uf)


def _ffn_gu_kernel(be_ref, nv_ref, xs_ref, w_ref, bg_ref, bu_ref, act_ref, wg_s, wu_s):
    b = pl.program_id(1)
    valid = b < nv_ref[0]
    new_w = jnp.logical_or(b == 0, be_ref[b] != be_ref[jnp.maximum(b - 1, 0)])

    @pl.when(jnp.logical_and(valid, new_w))
    def _():
        r_i = lax.broadcasted_iota(jnp.int32, (PERM_W, PERM_W), 0)
        c_i = lax.broadcasted_iota(jnp.int32, (PERM_W, PERM_W), 1)
        src = jnp.where(c_i < PERM_W // 2, 2 * c_i, 2 * (c_i - PERM_W // 2) + 1)
        perm = jnp.where(r_i == src, 1.0, 0.0).astype(BF16)
        hw = PERM_W // 2
        for c in range(2 * TF // PERM_W):
            chunk = w_ref[0, :, c * PERM_W:(c + 1) * PERM_W].astype(BF16)
            res = jnp.dot(chunk, perm, preferred_element_type=F32).astype(BF16)
            wg_s[:, c * hw:(c + 1) * hw] = res[:, 0:hw]
            wu_s[:, c * hw:(c + 1) * hw] = res[:, hw:PERM_W]

    @pl.when(valid)
    def _():
        x = jnp.concatenate(
            [xs_ref[pl.ds(s, M_BLK, stride=ROW_TILES), :].astype(BF16) for s in range(ROW_TILES)], axis=-1)
        gate = jnp.dot(x, wg_s[...], preferred_element_type=F32) + bg_ref[0]
        up = jnp.dot(x, wu_s[...], preferred_element_type=F32) + bu_ref[0]
        gate = jnp.minimum(gate, SWIGLU_LIMIT)
        up = jnp.clip(up, -SWIGLU_LIMIT, SWIGLU_LIMIT)
        act = (up + 1.0) * gate * jax.nn.sigmoid(SWIGLU_ALPHA * gate)
        act_ref[...] = act.astype(BF16)

    @pl.when(b >= nv_ref[0])
    def _():
        act_ref[...] = jnp.zeros(act_ref.shape, BF16)


def _ffn_down_kernel(be_ref, nv_ref, act_ref, wd_ref, bd_ref, y_ref):
    b = pl.program_id(0)

    @pl.when(b < nv_ref[0])
    def _():
        y = jnp.dot(act_ref[...], wd_ref[0], preferred_element_type=F32) + bd_ref[0]
        for s in range(ROW_TILES):
            y_ref[pl.ds(s, M_BLK, stride=ROW_TILES), :] = y[:, s * LANES:(s + 1) * LANES]

    @pl.when(b >= nv_ref[0])
    def _():
        y_ref[...] = jnp.zeros(y_ref.shape, F32)


def _blk(b, nv):
    return jnp.minimum(b, nv[0] - 1)


def _ffn_gu(block_e, nvalid, xs, w_gu, bg, bu):
    n_rows = xs.shape[0] // ROW_TILES
    nb = n_rows // M_BLK
    grid_spec = pltpu.PrefetchScalarGridSpec(
        num_scalar_prefetch=2,
        grid=(D_FF // TF, nb),
        in_specs=[
            pl.BlockSpec((M_BLK * ROW_TILES, LANES), lambda n, b, be, nv: (_blk(b, nv), 0)),
            pl.BlockSpec((1, D_MODEL, 2 * TF), lambda n, b, be, nv: (be[_blk(b, nv)], 0, n)),
            pl.BlockSpec((1, 1, TF), lambda n, b, be, nv: (be[_blk(b, nv)], 0, n)),
            pl.BlockSpec((1, 1, TF), lambda n, b, be, nv: (be[_blk(b, nv)], 0, n)),
        ],
        out_specs=pl.BlockSpec((M_BLK, TF), lambda n, b, be, nv: (b, n)),
        scratch_shapes=[pltpu.VMEM((D_MODEL, TF), BF16), pltpu.VMEM((D_MODEL, TF), BF16)],
    )
    return pl.pallas_call(
        _ffn_gu_kernel,
        grid_spec=grid_spec,
        out_shape=jax.ShapeDtypeStruct((n_rows, D_FF), BF16),
        compiler_params=_cparams(("arbitrary", "arbitrary")),
    )(block_e, nvalid, xs, w_gu, bg, bu)


def _ffn_down(block_e, nvalid, act, wd, bd):
    n_rows = act.shape[0]
    nb = n_rows // M_BLK
    grid_spec = pltpu.PrefetchScalarGridSpec(
        num_scalar_prefetch=2,
        grid=(nb,),
        in_specs=[
            pl.BlockSpec((M_BLK, D_FF), lambda b, be, nv: (_blk(b, nv), 0)),
            pl.BlockSpec((1, D_FF, D_MODEL), lambda b, be, nv: (be[_blk(b, nv)], 0, 0)),
            pl.BlockSpec((1, 1, D_MODEL), lambda b, be, nv: (be[_blk(b, nv)], 0, 0)),
        ],
        out_specs=pl.BlockSpec((M_BLK * ROW_TILES, LANES), lambda b, be, nv: (b, 0)),
    )
    return pl.pallas_call(
        _ffn_down_kernel,
        grid_spec=grid_spec,
        out_shape=jax.ShapeDtypeStruct((n_rows * ROW_TILES, LANES), F32),
        compiler_params=_cparams(("arbitrary",)),
    )(block_e, nvalid, act, wd, bd)


def _combine_kernel(dest_ref, ys_hbm, h_ref, w_ref, gfin_ref, o_ref, buf, hbuf, sems):
    rows_c = TOP_K * CT_COMB

    def copy(c, j):
        r = c * rows_c + j
        src = pl.multiple_of(dest_ref[r] * ROW_TILES, ROW_TILES)
        dst = pl.multiple_of(r * ROW_TILES, ROW_TILES)
        return pltpu.make_async_copy(ys_hbm.at[pl.ds(src, ROW_TILES)], buf.at[pl.ds(dst, ROW_TILES)],
                                     sems.at[c])

    for c in range(TB // CT_COMB):
        def issue(j, carry, c=c):
            copy(c, j).start()
            return carry
        lax.fori_loop(0, rows_c, issue, 0, unroll=8)

    for c in range(TB // CT_COMB):
        def drain(j, carry, c=c):
            copy(c, j).wait()
            return carry
        lax.fori_loop(0, rows_c, drain, 0, unroll=8)

        t0 = c * CT_COMB
        w = w_ref[t0:t0 + CT_COMB, :]
        for s in range(ROW_TILES):
            hs = h_ref[t0:t0 + CT_COMB, s * LANES:(s + 1) * LANES]
            for k in range(TOP_K):
                first = (c * rows_c + k * CT_COMB) * ROW_TILES + s
                hs = hs + w[:, k:k + 1] * buf[pl.ds(first, CT_COMB, stride=ROW_TILES), :]
            hbuf[t0:t0 + CT_COMB, s * LANES:(s + 1) * LANES] = hs
        h = hbuf[t0:t0 + CT_COMB, :]
        ms = jnp.mean(h * h, axis=-1, keepdims=True)
        o_ref[t0:t0 + CT_COMB, :] = h * lax.rsqrt(ms + NORM_EPS) * gfin_ref[...]


def _combine(dest_kmajor, ys, h1, wts, g_final):
    T = h1.shape[0]
    row = lambda i: (i, 0)
    return pl.pallas_call(
        _combine_kernel,
        grid=(T // TB,),
        in_specs=[
            pl.BlockSpec((TB * TOP_K,), lambda i: (i,), memory_space=pltpu.SMEM),
            pl.BlockSpec(memory_space=pl.ANY),
            pl.BlockSpec((TB, D_MODEL), row),
            pl.BlockSpec((TB, TOP_K), row),
            pl.BlockSpec((1, D_MODEL), lambda i: (0, 0)),
        ],
        out_specs=pl.BlockSpec((TB, D_MODEL), row),
        out_shape=jax.ShapeDtypeStruct((T, D_MODEL), F32),
        scratch_shapes=[pltpu.VMEM((TB * TOP_K * ROW_TILES, LANES), F32),
                        pltpu.VMEM((TB, D_MODEL), F32), pltpu.SemaphoreType.DMA((TB // CT_COMB,))],
        compiler_params=_cparams(("arbitrary",)),
    )(dest_kmajor, ys, h1, wts, g_final)


def _rope_tables(positions):
    half = ROT_DIM // 2
    inv = ROPE_THETA ** (-jnp.arange(0, ROT_DIM, 2, dtype=F32) / ROT_DIM)
    ang = positions.astype(F32).reshape(-1, 1) * inv
    cos, sin = jnp.cos(ang), jnp.sin(ang)
    T = ang.shape[0]
    rest = HEAD_DIM - ROT_DIM
    c64 = jnp.concatenate([cos, cos, jnp.ones((T, rest), F32)], axis=-1)
    s1_64 = jnp.concatenate([-sin, jnp.zeros((T, HEAD_DIM - half), F32)], axis=-1)
    s2_64 = jnp.concatenate([jnp.zeros((T, half), F32), sin, jnp.zeros((T, rest), F32)], axis=-1)
    dup = lambda t: jnp.concatenate([t, t], axis=-1)
    return dup(c64), dup(s1_64), dup(s2_64)


def kernel(x, positions, norm_mix_g, w_in, gate_b, conv_dw_w, conv_dw_b, conv_ln_g, conv_ln_b, w_conv_out, lambda_q1, lambda_k1, lambda_q2, lambda_k2, attn_subln_g, w_attn_out, w_out, norm_ffn_g, router_w, router_b, w_gu, b_gu, w_down, b_down, norm_final_g):
    B, S, D = x.shape
    T = B * S
    x2 = x.reshape(T, D)
    ctab, s1tab, s2tab = _rope_tables(positions)

    proj = _inproj(x2, norm_mix_g[0].reshape(1, D), w_in[0].astype(BF16), ctab, s1tab, s2tab)
    m_a = _conv_branch(proj, B, S, conv_dw_w[0], conv_dw_b[0].reshape(1, -1),
                       conv_ln_g[0].reshape(1, -1), conv_ln_b[0].reshape(1, -1),
                       w_conv_out[0].astype(BF16), gate_b[0, :D].reshape(1, D))
    lam_vecs = jnp.stack([lambda_q1[0], lambda_k1[0], lambda_q2[0], lambda_k2[0]])
    attn_o = _attention(proj, B, S, lam_vecs, attn_subln_g[0].reshape(1, -1))
    h1, uf, logits_t = _mix(attn_o, proj, gate_b[0, D:].reshape(1, D), m_a, x2,
                            w_attn_out[0].astype(BF16), w_out[0].astype(BF16),
                            norm_ffn_g[0].reshape(1, D), router_w[0].T, router_b[0].reshape(-1, 1))
    logits = logits_t.T

    idx, wts, rank, counts = _route(logits)
    counts = counts[0].astype(jnp.int32)
    padded = (counts + M_BLK - 1) // M_BLK * M_BLK
    pad_end = jnp.cumsum(padded)
    pad_start = pad_end - padded
    dest = pad_start[idx] + rank
    n_blocks = (T * TOP_K) // M_BLK + N_EXPERTS
    n_rows = n_blocks * M_BLK
    blk_start = jnp.arange(n_blocks, dtype=jnp.int32) * M_BLK
    block_e = jnp.minimum(jnp.sum((pad_end[None, :] <= blk_start[:, None]).astype(jnp.int32), axis=1),
                          N_EXPERTS - 1)
    nvalid = (pad_end[-1:] // M_BLK).astype(jnp.int32)

    pad_lo = jnp.concatenate([pad_start + counts, nvalid]).astype(jnp.int32)
    pad_hi = jnp.concatenate([pad_end, jnp.full((1,), n_blocks)]).astype(jnp.int32)
    xs = _dispatch(pad_lo, pad_hi, dest.reshape(-1), uf, n_rows)
    bg = b_gu[0, :, 0::2].reshape(N_EXPERTS, 1, D_FF)
    bu = b_gu[0, :, 1::2].reshape(N_EXPERTS, 1, D_FF)
    act = _ffn_gu(block_e, nvalid, xs, w_gu[0], bg, bu)
    ys = _ffn_down(block_e, nvalid, act, w_down[0].astype(BF16), b_down[0].reshape(N_EXPERTS, 1, D))

    dest_km = dest.reshape(T // CT_COMB, CT_COMB, TOP_K).transpose(0, 2, 1).reshape(-1)
    out = _combine(dest_km, ys, h1, wts, norm_final_g.reshape(1, D))
    return out.reshape(B, S, D)
```

```python
import functools

import jax
import jax.numpy as jnp
from jax import lax
from jax.experimental import pallas as pl
from jax.experimental.pallas import tpu as pltpu

F32 = jnp.float32
BF16 = jnp.bfloat16

D_MODEL = 2048
CHUNK = 64
CONV_DIM = 1024
CONV_WIDTH = 31
N_HEADS = 8
HEAD_DIM = 64
ATTN_DIM = N_HEADS * 2 * HEAD_DIM
ROT_DIM = HEAD_DIM // 4
ROPE_THETA = 500000.0
N_BRANCH = 2
IN_COLS = 2 * CONV_DIM + 3 * ATTN_DIM + N_BRANCH * D_MODEL
N_EXPERTS = 32
TOP_K = 4
D_FF = 2048
SWIGLU_LIMIT = 7.0
SWIGLU_ALPHA = 1.702
NORM_EPS = 1e-6
LN_EPS = 1e-5
LAM_INIT = 0.2

LANES = 128
SUBLANES = 8
VMEM_LIMIT = 56 * 1024 * 1024

COL_Q = 2 * CONV_DIM
COL_K = COL_Q + ATTN_DIM
COL_V = COL_K + ATTN_DIM
COL_G = COL_V + ATTN_DIM

TM_IN = 1024
TN_IN = 1024
TS_CONV = 256
HALO = 32
TQ = 512
TK = 512
RG = 128
KV_UNROLL = 4
assert KV_UNROLL == 4 and RG % CHUNK == 0 and TQ == TK
TM_MIX = 512
RG_MIX = 128
TR = 512
TB = 256
CT_COMB = 64
M_BLK = 512
TF = 1024
CAST_ROWS = 256
ROW_TILES = D_MODEL // LANES
PERM_W = 256


def _cparams(sem):
    return pltpu.CompilerParams(dimension_semantics=sem, vmem_limit_bytes=VMEM_LIMIT)


def _inproj_kernel(x_ref, g_ref, w_ref, c_ref, s1_ref, s2_ref, o_ref, u_ref):
    j = pl.program_id(1)

    @pl.when(j == 0)
    def _():
        x = x_ref[...]
        ms = jnp.mean(x * x, axis=-1, keepdims=True)
        u_ref[...] = (x * lax.rsqrt(ms + NORM_EPS) * g_ref[...]).astype(BF16)

    acc = jnp.dot(u_ref[...], w_ref[...], preferred_element_type=F32)
    q_tile = COL_Q // TN_IN
    k_tile = COL_K // TN_IN
    is_rope = jnp.logical_or(j == q_tile, j == k_tile)

    @pl.when(is_rope)
    def _():
        scale = jnp.where(j == q_tile, HEAD_DIM ** -0.5, 1.0).astype(F32)
        c = c_ref[...]
        s1 = s1_ref[...]
        s2 = s2_ref[...]
        for hh in range(TN_IN // LANES):
            t = acc[:, hh * LANES:(hh + 1) * LANES]
            r = (t * c + pltpu.roll(t, LANES - ROT_DIM // 2, 1) * s1
                 + pltpu.roll(t, ROT_DIM // 2, 1) * s2)
            o_ref[:, hh * LANES:(hh + 1) * LANES] = (r * scale).astype(BF16)

    @pl.when(jnp.logical_not(is_rope))
    def _():
        o_ref[...] = acc.astype(BF16)


def _inproj(x2, g, w_bf, ctab, s1tab, s2tab):
    T = x2.shape[0]
    return pl.pallas_call(
        _inproj_kernel,
        grid=(T // TM_IN, IN_COLS // TN_IN),
        in_specs=[
            pl.BlockSpec((TM_IN, D_MODEL), lambda i, j: (i, 0)),
            pl.BlockSpec((1, D_MODEL), lambda i, j: (0, 0)),
            pl.BlockSpec((D_MODEL, TN_IN), lambda i, j: (0, j)),
            pl.BlockSpec((TM_IN, LANES), lambda i, j: (i, 0)),
            pl.BlockSpec((TM_IN, LANES), lambda i, j: (i, 0)),
            pl.BlockSpec((TM_IN, LANES), lambda i, j: (i, 0)),
        ],
        out_specs=pl.BlockSpec((TM_IN, TN_IN), lambda i, j: (i, j)),
        out_shape=jax.ShapeDtypeStruct((T, IN_COLS), BF16),
        scratch_shapes=[pltpu.VMEM((TM_IN, D_MODEL), BF16)],
        compiler_params=_cparams(("parallel", "arbitrary")),
    )(x2, g, w_bf, ctab, s1tab, s2tab)


def _conv_kernel(val_ref, gate_ref, hval_ref, hgate_ref, dww_ref, dwb_ref, lng_ref, lnb_ref,
                 wco_ref, ga0_ref, ga1_ref, gab_ref, o_ref, hbuf, cbuf, shbuf):
    i = pl.program_id(1)
    ts = TS_CONV
    halo = hval_ref[...].astype(F32) * jax.nn.sigmoid(hgate_ref[...].astype(F32))
    hbuf[0:HALO, :] = jnp.where(i == 0, 0.0, halo)
    hbuf[HALO:HALO + ts, :] = val_ref[...].astype(F32) * jax.nn.sigmoid(gate_ref[...].astype(F32))

    base = HALO - (CONV_WIDTH - 1)

    def lane_chunk(c, carry):
        l0 = pl.multiple_of(c * LANES, LANES)
        acc = jnp.zeros((ts, LANES), F32) + dwb_ref[:, pl.ds(l0, LANES)]
        for r in range(SUBLANES):
            taps = [j for j in range(CONV_WIDTH) if (base + j) % SUBLANES == r]
            q_max = max((base + j) // SUBLANES for j in taps)
            n = SUBLANES * q_max + ts
            shbuf[0:n, :] = hbuf[r:r + n, pl.ds(l0, LANES)]
            for j in taps:
                q = (base + j) // SUBLANES
                acc = acc + dww_ref[j:j + 1, pl.ds(l0, LANES)] * shbuf[SUBLANES * q:SUBLANES * q + ts, :]
        cbuf[:, pl.ds(l0, LANES)] = acc
        return carry

    lax.fori_loop(0, CONV_DIM // LANES, lane_chunk, 0)

    y = cbuf[...]
    mu = jnp.mean(y, axis=-1, keepdims=True)
    yc = y - mu
    var = jnp.mean(yc * yc, axis=-1, keepdims=True)
    yn = yc * lax.rsqrt(var + LN_EPS) * lng_ref[...] + lnb_ref[...]
    a = yn * jax.nn.sigmoid(yn)
    out = jnp.dot(a.astype(BF16), wco_ref[...], preferred_element_type=F32)
    half = D_MODEL // 2
    g0 = jax.nn.sigmoid(ga0_ref[...].astype(F32) + gab_ref[:, 0:half])
    g1 = jax.nn.sigmoid(ga1_ref[...].astype(F32) + gab_ref[:, half:D_MODEL])
    o_ref[:, 0:half] = (out[:, 0:half] * g0).astype(BF16)
    o_ref[:, half:D_MODEL] = (out[:, half:D_MODEL] * g1).astype(BF16)


def _conv_branch(proj, B, S, dww, dwb, lng, lnb, wco_bf, gate_b_a):
    T = B * S
    nts = S // TS_CONV
    hpt = TS_CONV // HALO
    gcol = COL_G // CONV_DIM

    def halo_map(col):
        return lambda b, i: (jnp.maximum((b * nts + i) * hpt - 1, 0), col)

    const = lambda b, i: (0, 0)
    return pl.pallas_call(
        _conv_kernel,
        grid=(B, nts),
        in_specs=[
            pl.BlockSpec((TS_CONV, CONV_DIM), lambda b, i: (b * nts + i, 0)),
            pl.BlockSpec((TS_CONV, CONV_DIM), lambda b, i: (b * nts + i, 1)),
            pl.BlockSpec((HALO, CONV_DIM), halo_map(0)),
            pl.BlockSpec((HALO, CONV_DIM), halo_map(1)),
            pl.BlockSpec((CONV_WIDTH, CONV_DIM), const),
            pl.BlockSpec((1, CONV_DIM), const),
            pl.BlockSpec((1, CONV_DIM), const),
            pl.BlockSpec((1, CONV_DIM), const),
            pl.BlockSpec((CONV_DIM, D_MODEL), const),
            pl.BlockSpec((TS_CONV, CONV_DIM), lambda b, i: (b * nts + i, gcol)),
            pl.BlockSpec((TS_CONV, CONV_DIM), lambda b, i: (b * nts + i, gcol + 1)),
            pl.BlockSpec((1, D_MODEL), const),
        ],
        out_specs=pl.BlockSpec((TS_CONV, D_MODEL), lambda b, i: (b * nts + i, 0)),
        out_shape=jax.ShapeDtypeStruct((T, D_MODEL), BF16),
        scratch_shapes=[pltpu.VMEM((HALO + TS_CONV, CONV_DIM), F32),
                        pltpu.VMEM((TS_CONV, CONV_DIM), F32),
                        pltpu.VMEM((HALO + TS_CONV, LANES), F32)],
        compiler_params=_cparams(("parallel", "arbitrary")),
    )(proj, proj, proj, proj, dww, dwb, lng, lnb, wco_bf, proj, proj, gate_b_a)


def _attn_kernel(q_ref, k_ref, v_ref, lam_ref, sg_ref, o_ref, qs_ref, vx_ref, m_ref, acc_ref):
    qi = pl.program_id(2)
    hw = 2 * HEAD_DIM

    @pl.when(qi == 0)
    def _():
        vx_ref[:, 0:hw] = v_ref[...]
        vx_ref[:, hw:2 * hw] = jnp.ones((v_ref.shape[0], hw), BF16)

    q = q_ref[...]
    lane = lax.broadcasted_iota(jnp.int32, (TQ, hw), 1)
    zero = jnp.zeros_like(q)
    qs_ref[0:TQ, :] = jnp.where(lane < HEAD_DIM, q, zero)
    qs_ref[TQ:2 * TQ, :] = jnp.where(lane >= HEAD_DIM, q, zero)
    m_ref[...] = jnp.full(m_ref.shape, -jnp.inf, F32)
    acc_ref[...] = jnp.zeros(acc_ref.shape, F32)

    def step(kc, masked):
        k0 = pl.multiple_of(kc * TK, TK)
        kb = k_ref[pl.ds(k0, TK), :]
        vb = vx_ref[pl.ds(k0, TK), :]
        for g in range(2 * TQ // RG):
            r0 = g * RG
            s = lax.dot_general(qs_ref[r0:r0 + RG, :], kb, (((1,), (1,)), ((), ())),
                                preferred_element_type=F32)
            if masked:
                row = lax.broadcasted_iota(jnp.int32, (RG, TK), 0) + (r0 % TQ)
                col = lax.broadcasted_iota(jnp.int32, (RG, TK), 1)
                s = jnp.where(col // CHUNK <= row // CHUNK, s, -jnp.inf)
            m_old = m_ref[r0:r0 + RG, :]
            m_new = jnp.maximum(m_old, jnp.max(s, axis=-1, keepdims=True))
            alpha = jnp.exp(m_old - m_new)
            p = jnp.exp(s - jnp.tile(m_new, (1, TK // LANES)))
            pv = jnp.dot(p.astype(BF16), vb, preferred_element_type=F32)
            acc_ref[r0:r0 + RG, :] = jnp.tile(alpha, (1, 2)) * acc_ref[r0:r0 + RG, :] + pv
            m_ref[r0:r0 + RG, :] = m_new

    def body(j, carry):
        for u in range(KV_UNROLL):
            step(KV_UNROLL * j + u, False)
        return carry

    n_full = qi // KV_UNROLL
    lax.fori_loop(0, n_full, body, 0)
    rem = qi - n_full * KV_UNROLL

    @pl.when(rem >= 2)
    def _():
        step(n_full * KV_UNROLL, False)
        step(n_full * KV_UNROLL + 1, False)

    @pl.when(rem % 2 == 1)
    def _():
        step(qi - 1, False)

    step(qi, True)

    lam_v = lam_ref[...]
    lam = (jnp.exp(jnp.sum(lam_v[0:1, :] * lam_v[1:2, :], axis=-1, keepdims=True))
           - jnp.exp(jnp.sum(lam_v[2:3, :] * lam_v[3:4, :], axis=-1, keepdims=True)) + LAM_INIT)
    o0 = acc_ref[0:TQ, 0:hw] / acc_ref[0:TQ, hw:2 * hw]
    o1 = acc_ref[TQ:2 * TQ, 0:hw] / acc_ref[TQ:2 * TQ, hw:2 * hw]
    o = o0 - lam * o1
    ms = jnp.mean(o * o, axis=-1, keepdims=True)
    o = o * lax.rsqrt(ms + NORM_EPS) * sg_ref[...] * (1.0 - LAM_INIT)
    o_ref[...] = o.astype(BF16)


def _attention(proj, B, S, lam_vecs, subln_g):
    T = B * S
    nq = S // TQ
    hw = 2 * HEAD_DIM
    qc, kc, vc = COL_Q // hw, COL_K // hw, COL_V // hw
    return pl.pallas_call(
        _attn_kernel,
        grid=(B, N_HEADS, nq),
        in_specs=[
            pl.BlockSpec((TQ, hw), lambda b, h, i: (b * nq + i, qc + h)),
            pl.BlockSpec((S, hw), lambda b, h, i: (b, kc + h)),
            pl.BlockSpec((S, hw), lambda b, h, i: (b, vc + h)),
            pl.BlockSpec((4, HEAD_DIM), lambda b, h, i: (0, 0)),
            pl.BlockSpec((1, hw), lambda b, h, i: (0, 0)),
        ],
        out_specs=pl.BlockSpec((TQ, hw), lambda b, h, i: (b * nq + i, h)),
        out_shape=jax.ShapeDtypeStruct((T, ATTN_DIM), BF16),
        scratch_shapes=[pltpu.VMEM((2 * TQ, hw), BF16),
                        pltpu.VMEM((S, 2 * hw), BF16),
                        pltpu.VMEM((2 * TQ, LANES), F32),
                        pltpu.VMEM((2 * TQ, 2 * hw), F32)],
        compiler_params=_cparams(("parallel", "parallel", "arbitrary")),
    )(proj, proj, proj, lam_vecs, subln_g)


def _mix_kernel(a_ref, gb0_ref, gb1_ref, gbb_ref, ma_ref, x_ref, wao_ref, wo_ref, gf_ref,
                rwt_ref, rb_ref, h_ref, uf_ref, lg_ref):
    half = D_MODEL // 2
    for g in range(TM_MIX // RG_MIX):
        r0 = g * RG_MIX
        yb = jnp.dot(a_ref[r0:r0 + RG_MIX, :], wao_ref[...], preferred_element_type=F32)
        g0 = jax.nn.sigmoid(gb0_ref[r0:r0 + RG_MIX, :].astype(F32) + gbb_ref[:, 0:half])
        g1 = jax.nn.sigmoid(gb1_ref[r0:r0 + RG_MIX, :].astype(F32) + gbb_ref[:, half:D_MODEL])
        m0 = ma_ref[r0:r0 + RG_MIX, 0:half].astype(F32) + g0 * yb[:, 0:half]
        m1 = ma_ref[r0:r0 + RG_MIX, half:D_MODEL].astype(F32) + g1 * yb[:, half:D_MODEL]
        hh = (jnp.dot(m0.astype(BF16), wo_ref[0:half, :], preferred_element_type=F32)
              + jnp.dot(m1.astype(BF16), wo_ref[half:D_MODEL, :], preferred_element_type=F32))
        h = x_ref[r0:r0 + RG_MIX, :] + hh
        h_ref[r0:r0 + RG_MIX, :] = h
        ms = jnp.mean(h * h, axis=-1, keepdims=True)
        u = h * lax.rsqrt(ms + NORM_EPS) * gf_ref[...]
        for s in range(ROW_TILES):
            uf_ref[pl.ds(r0 * ROW_TILES + s, RG_MIX, stride=ROW_TILES), :] = u[:, s * LANES:(s + 1) * LANES]
        lg_ref[:, r0:r0 + RG_MIX] = lax.dot_general(
            rwt_ref[...], u, (((1,), (1,)), ((), ())), preferred_element_type=F32,
            precision=lax.Precision.HIGHEST) + rb_ref[...]


def _mix(attn_o, proj, gate_b_b, m_a, x2, wao_bf, wo_bf, g_ffn, router_w_t, router_b_col):
    T = x2.shape[0]
    gcol = (COL_G + D_MODEL) // CONV_DIM
    const = lambda i: (0, 0)
    row = lambda i: (i, 0)
    return pl.pallas_call(
        _mix_kernel,
        grid=(T // TM_MIX,),
        in_specs=[
            pl.BlockSpec((TM_MIX, ATTN_DIM), row),
            pl.BlockSpec((TM_MIX, CONV_DIM), lambda i: (i, gcol)),
            pl.BlockSpec((TM_MIX, CONV_DIM), lambda i: (i, gcol + 1)),
            pl.BlockSpec((1, D_MODEL), const),
            pl.BlockSpec((TM_MIX, D_MODEL), row),
            pl.BlockSpec((TM_MIX, D_MODEL), row),
            pl.BlockSpec((ATTN_DIM, D_MODEL), const, pipeline_mode=pl.Buffered(1)),
            pl.BlockSpec((D_MODEL, D_MODEL), const, pipeline_mode=pl.Buffered(1)),
            pl.BlockSpec((1, D_MODEL), const),
            pl.BlockSpec((N_EXPERTS, D_MODEL), const),
            pl.BlockSpec((N_EXPERTS, 1), const),
        ],
        out_specs=[
            pl.BlockSpec((TM_MIX, D_MODEL), row),
            pl.BlockSpec((TM_MIX * ROW_TILES, LANES), row),
            pl.BlockSpec((N_EXPERTS, TM_MIX), lambda i: (0, i)),
        ],
        out_shape=[
            jax.ShapeDtypeStruct((T, D_MODEL), F32),
            jax.ShapeDtypeStruct((T * ROW_TILES, LANES), F32),
            jax.ShapeDtypeStruct((N_EXPERTS, T), F32),
        ],
        compiler_params=_cparams(("parallel",)),
    )(attn_o, proj, proj, gate_b_b, m_a, x2, wao_bf, wo_bf, g_ffn, router_w_t, router_b_col)


def _route_kernel(lg_ref, idx_ref, w_ref, rank_ref, cnt_ref, carry_ref):
    i = pl.program_id(0)

    @pl.when(i == 0)
    def _():
        carry_ref[...] = jnp.zeros(carry_ref.shape, F32)

    vals = lg_ref[...]
    lane = lax.broadcasted_iota(jnp.int32, (TR, N_EXPERTS), 1)
    sel = jnp.zeros((TR, N_EXPERTS), F32)
    tops, idxs, hots = [], [], []
    for _ in range(TOP_K):
        m = jnp.max(vals, axis=-1, keepdims=True)
        ix = jnp.min(jnp.where(vals == m, lane, N_EXPERTS), axis=-1, keepdims=True)
        hot = lane == ix
        vals = jnp.where(hot, -jnp.inf, vals)
        sel = sel + hot.astype(F32)
        tops.append(m)
        idxs.append(ix)
        hots.append(hot)
    es = [jnp.exp(t - tops[0]) for t in tops]
    den = es[0] + es[1] + es[2] + es[3]

    r_i = lax.broadcasted_iota(jnp.int32, (TR, TR), 0)
    c_i = lax.broadcasted_iota(jnp.int32, (TR, TR), 1)
    tri = jnp.where(c_i < r_i, 1.0, 0.0).astype(BF16)
    before = jnp.dot(tri, sel.astype(BF16), preferred_element_type=F32) + carry_ref[...]
    carry_ref[...] = carry_ref[...] + jnp.sum(sel, axis=0, keepdims=True)
    cnt_ref[...] = carry_ref[...]

    lane4 = lax.broadcasted_iota(jnp.int32, (TR, TOP_K), 1)
    idx_o = jnp.zeros((TR, TOP_K), jnp.int32)
    w_o = jnp.zeros((TR, TOP_K), F32)
    rk_o = jnp.zeros((TR, TOP_K), F32)
    for k in range(TOP_K):
        rk = jnp.sum(jnp.where(hots[k], before, 0.0), axis=-1, keepdims=True)
        idx_o = jnp.where(lane4 == k, idxs[k], idx_o)
        w_o = jnp.where(lane4 == k, es[k] / den, w_o)
        rk_o = jnp.where(lane4 == k, rk, rk_o)
    idx_ref[...] = idx_o
    w_ref[...] = w_o
    rank_ref[...] = rk_o.astype(jnp.int32)


def _route(logits):
    T = logits.shape[0]
    row = lambda i: (i, 0)
    return pl.pallas_call(
        _route_kernel,
        grid=(T // TR,),
        in_specs=[pl.BlockSpec((TR, N_EXPERTS), row)],
        out_specs=[
            pl.BlockSpec((TR, TOP_K), row),
            pl.BlockSpec((TR, TOP_K), row),
            pl.BlockSpec((TR, TOP_K), row),
            pl.BlockSpec((1, N_EXPERTS), lambda i: (0, 0)),
        ],
        out_shape=[
            jax.ShapeDtypeStruct((T, TOP_K), jnp.int32),
            jax.ShapeDtypeStruct((T, TOP_K), F32),
            jax.ShapeDtypeStruct((T, TOP_K), jnp.int32),
            jax.ShapeDtypeStruct((1, N_EXPERTS), F32),
        ],
        scratch_shapes=[pltpu.VMEM((1, N_EXPERTS), F32)],
        compiler_params=_cparams(("arbitrary",)),
    )(logits)


def _dispatch_kernel(plo_ref, phi_ref, dest_ref, u_ref, xs_hbm, zbuf, sem, zsem):
    i = pl.program_id(0)

    @pl.when(i == 0)
    def _():
        zbuf[...] = jnp.zeros(zbuf.shape, F32)

    def copy(t, k):
        src = pl.multiple_of(t * ROW_TILES, ROW_TILES)
        dst = pl.multiple_of(dest_ref[t * TOP_K + k] * ROW_TILES, ROW_TILES)
        return pltpu.make_async_copy(u_ref.at[pl.ds(src, ROW_TILES)], xs_hbm.at[pl.ds(dst, ROW_TILES)], sem)

    def issue(t, carry):
        for k in range(TOP_K):
            copy(t, k).start()
        return carry

    def drain(t, carry):
        for k in range(TOP_K):
            copy(t, k).wait()
        return carry

    lax.fori_loop(0, TB, issue, 0, unroll=4)

    def zrow(r):
        dst = pl.multiple_of(r * ROW_TILES, ROW_TILES)
        return pltpu.make_async_copy(zbuf.at[pl.ds(0, ROW_TILES)], xs_hbm.at[pl.ds(dst, ROW_TILES)], zsem)

    def zblock(b):
        dst = pl.multiple_of(b * (M_BLK * ROW_TILES), M_BLK * ROW_TILES)
        return pltpu.make_async_copy(zbuf, xs_hbm.at[pl.ds(dst, M_BLK * ROW_TILES)], zsem)

    def span(fn):
        def go(r, carry):
            fn(r)
            return carry
        return go

    e = jnp.minimum(i, N_EXPERTS)
    lo = plo_ref[e]
    hi = phi_ref[e]
    n_row = jnp.where(i < N_EXPERTS, hi, lo)
    n_blk = jnp.where(i == N_EXPERTS, hi, lo)
    lax.fori_loop(lo, n_row, span(lambda r: zrow(r).start()), 0)
    lax.fori_loop(lo, n_blk, span(lambda b: zblock(b).start()), 0)
    lax.fori_loop(0, TB, drain, 0, unroll=4)
    lax.fori_loop(lo, n_row, span(lambda r: zrow(r).wait()), 0)
    lax.fori_loop(lo, n_blk, span(lambda b: zblock(b).wait()), 0)


def _dispatch(pad_lo, pad_hi, dest_flat, uf, n_rows):
    T = uf.shape[0] // ROW_TILES
    assert T // TB > N_EXPERTS
    grid_spec = pltpu.PrefetchScalarGridSpec(
        num_scalar_prefetch=2,
        grid=(T // TB,),
        in_specs=[
            pl.BlockSpec((TB * TOP_K,), lambda i, lo, hi: (i,), memory_space=pltpu.SMEM),
            pl.BlockSpec((TB * ROW_TILES, LANES), lambda i, lo, hi: (i, 0)),
        ],
        out_specs=pl.BlockSpec(memory_space=pl.ANY),
        scratch_shapes=[pltpu.VMEM((M_BLK * ROW_TILES, LANES), F32), pltpu.SemaphoreType.DMA,
                        pltpu.SemaphoreType.DMA],
    )
    return pl.pallas_call(
        _dispatch_kernel,
        grid_spec=grid_spec,
        out_shape=jax.ShapeDtypeStruct((n_rows * ROW_TILES, LANES), F32),
        compiler_params=_cparams(("arbitrary",)),
    )(pad_lo, pad_hi, dest_flat, uf)


def _ffn_gu_kernel(be_ref, nv_ref, nxt_ref, xs_ref, w_hbm, bg_ref, bu_ref, act_ref, stage, wg_s, wu_s, sem):
    n = pl.program_id(0)
    b = pl.program_id(1)
    valid = b < nv_ref[0]
    e = be_ref[b]
    new_w = jnp.logical_or(b == 0, e != be_ref[jnp.maximum(b - 1, 0)])

    def fetch(expert, tile):
        c0 = pl.multiple_of(tile * (2 * TF), 2 * TF)
        return pltpu.make_async_copy(w_hbm.at[expert, :, pl.ds(c0, 2 * TF)], stage, sem)

    @pl.when(jnp.logical_and(n == 0, b == 0))
    def _():
        fetch(e, n).start()

    @pl.when(jnp.logical_and(valid, new_w))
    def _():
        fetch(e, n).wait()
        r_i = lax.broadcasted_iota(jnp.int32, (PERM_W, PERM_W), 0)
        c_i = lax.broadcasted_iota(jnp.int32, (PERM_W, PERM_W), 1)
        src = jnp.where(c_i < PERM_W // 2, 2 * c_i, 2 * (c_i - PERM_W // 2) + 1)
        perm = jnp.where(r_i == src, 1.0, 0.0).astype(BF16)
        hw = PERM_W // 2
        for c in range(2 * TF // PERM_W):
            chunk = stage[:, c * PERM_W:(c + 1) * PERM_W].astype(BF16)
            res = jnp.dot(chunk, perm, preferred_element_type=F32).astype(BF16)
            wg_s[:, c * hw:(c + 1) * hw] = res[:, 0:hw]
            wu_s[:, c * hw:(c + 1) * hw] = res[:, hw:PERM_W]
        nx = nxt_ref[b]

        @pl.when(nx >= 0)
        def _():
            fetch(nx, n).start()

        @pl.when(jnp.logical_and(nx < 0, n + 1 < D_FF // TF))
        def _():
            fetch(be_ref[0], n + 1).start()

    @pl.when(valid)
    def _():
        x = jnp.concatenate(
            [xs_ref[pl.ds(s, M_BLK, stride=ROW_TILES), :].astype(BF16) for s in range(ROW_TILES)], axis=-1)
        gate = jnp.dot(x, wg_s[...], preferred_element_type=F32) + bg_ref[0]
        up = jnp.dot(x, wu_s[...], preferred_element_type=F32) + bu_ref[0]
        gate = jnp.minimum(gate, SWIGLU_LIMIT)
        up = jnp.clip(up, -SWIGLU_LIMIT, SWIGLU_LIMIT)
        act = (up + 1.0) * gate * jax.nn.sigmoid(SWIGLU_ALPHA * gate)
        act_ref[...] = act.astype(BF16)

    @pl.when(b >= nv_ref[0])
    def _():
        act_ref[...] = jnp.zeros(act_ref.shape, BF16)


def _ffn_down_kernel(be_ref, nv_ref, nxt_ref, act_ref, wd_hbm, bd_ref, y_ref, stage, wd_s, sem):
    b = pl.program_id(0)
    valid = b < nv_ref[0]
    e = be_ref[b]
    new_w = jnp.logical_or(b == 0, e != be_ref[jnp.maximum(b - 1, 0)])

    def fetch(expert):
        return pltpu.make_async_copy(wd_hbm.at[expert], stage, sem)

    @pl.when(b == 0)
    def _():
        fetch(e).start()

    @pl.when(jnp.logical_and(valid, new_w))
    def _():
        fetch(e).wait()
        for r in range(D_FF // CAST_ROWS):
            wd_s[r * CAST_ROWS:(r + 1) * CAST_ROWS, :] = stage[r * CAST_ROWS:(r + 1) * CAST_ROWS, :].astype(BF16)
        nx = nxt_ref[b]

        @pl.when(nx >= 0)
        def _():
            fetch(nx).start()

    @pl.when(valid)
    def _():
        y = jnp.dot(act_ref[...], wd_s[...], preferred_element_type=F32) + bd_ref[0]
        for s in range(ROW_TILES):
            y_ref[pl.ds(s, M_BLK, stride=ROW_TILES), :] = y[:, s * LANES:(s + 1) * LANES]

    @pl.when(b >= nv_ref[0])
    def _():
        y_ref[...] = jnp.zeros(y_ref.shape, F32)


def _blk(b, nv):
    return jnp.minimum(b, nv[0] - 1)


def _ffn_gu(block_e, nvalid, nxt, xs, w_gu, bg, bu):
    n_rows = xs.shape[0] // ROW_TILES
    nb = n_rows // M_BLK
    grid_spec = pltpu.PrefetchScalarGridSpec(
        num_scalar_prefetch=3,
        grid=(D_FF // TF, nb),
        in_specs=[
            pl.BlockSpec((M_BLK * ROW_TILES, LANES), lambda n, b, be, nv, nx: (_blk(b, nv), 0)),
            pl.BlockSpec(memory_space=pl.ANY),
            pl.BlockSpec((1, 1, TF), lambda n, b, be, nv, nx: (be[_blk(b, nv)], 0, n)),
            pl.BlockSpec((1, 1, TF), lambda n, b, be, nv, nx: (be[_blk(b, nv)], 0, n)),
        ],
        out_specs=pl.BlockSpec((M_BLK, TF), lambda n, b, be, nv, nx: (b, n)),
        scratch_shapes=[pltpu.VMEM((D_MODEL, 2 * TF), F32), pltpu.VMEM((D_MODEL, TF), BF16),
                        pltpu.VMEM((D_MODEL, TF), BF16), pltpu.SemaphoreType.DMA],
    )
    return pl.pallas_call(
        _ffn_gu_kernel,
        grid_spec=grid_spec,
        out_shape=jax.ShapeDtypeStruct((n_rows, D_FF), BF16),
        compiler_params=_cparams(("arbitrary", "arbitrary")),
    )(block_e, nvalid, nxt, xs, w_gu, bg, bu)


def _ffn_down(block_e, nvalid, nxt, act, wd, bd):
    n_rows = act.shape[0]
    nb = n_rows // M_BLK
    grid_spec = pltpu.PrefetchScalarGridSpec(
        num_scalar_prefetch=3,
        grid=(nb,),
        in_specs=[
            pl.BlockSpec((M_BLK, D_FF), lambda b, be, nv, nx: (_blk(b, nv), 0)),
            pl.BlockSpec(memory_space=pl.ANY),
            pl.BlockSpec((1, 1, D_MODEL), lambda b, be, nv, nx: (be[_blk(b, nv)], 0, 0)),
        ],
        out_specs=pl.BlockSpec((M_BLK * ROW_TILES, LANES), lambda b, be, nv, nx: (b, 0)),
        scratch_shapes=[pltpu.VMEM((D_FF, D_MODEL), F32), pltpu.VMEM((D_FF, D_MODEL), BF16),
                        pltpu.SemaphoreType.DMA],
    )
    return pl.pallas_call(
        _ffn_down_kernel,
        grid_spec=grid_spec,
        out_shape=jax.ShapeDtypeStruct((n_rows * ROW_TILES, LANES), F32),
        compiler_params=_cparams(("arbitrary",)),
    )(block_e, nvalid, nxt, act, wd, bd)


def _combine_kernel(dest_ref, ys_hbm, h_ref, w_ref, gfin_ref, o_ref, buf, hbuf, sems):
    rows_c = TOP_K * CT_COMB

    def copy(c, j):
        r = c * rows_c + j
        src = pl.multiple_of(dest_ref[r] * ROW_TILES, ROW_TILES)
        dst = pl.multiple_of(r * ROW_TILES, ROW_TILES)
        return pltpu.make_async_copy(ys_hbm.at[pl.ds(src, ROW_TILES)], buf.at[pl.ds(dst, ROW_TILES)],
                                     sems.at[c])

    for c in range(TB // CT_COMB):
        def issue(j, carry, c=c):
            copy(c, j).start()
            return carry
        lax.fori_loop(0, rows_c, issue, 0, unroll=8)

    for c in range(TB // CT_COMB):
        def drain(j, carry, c=c):
            copy(c, j).wait()
            return carry
        lax.fori_loop(0, rows_c, drain, 0, unroll=8)

        t0 = c * CT_COMB
        w = w_ref[t0:t0 + CT_COMB, :]
        for s in range(ROW_TILES):
            hs = h_ref[t0:t0 + CT_COMB, s * LANES:(s + 1) * LANES]
            for k in range(TOP_K):
                first = (c * rows_c + k * CT_COMB) * ROW_TILES + s
                hs = hs + w[:, k:k + 1] * buf[pl.ds(first, CT_COMB, stride=ROW_TILES), :]
            hbuf[t0:t0 + CT_COMB, s * LANES:(s + 1) * LANES] = hs
        h = hbuf[t0:t0 + CT_COMB, :]
        ms = jnp.mean(h * h, axis=-1, keepdims=True)
        o_ref[t0:t0 + CT_COMB, :] = h * lax.rsqrt(ms + NORM_EPS) * gfin_ref[...]


def _combine(dest_kmajor, ys, h1, wts, g_final):
    T = h1.shape[0]
    row = lambda i: (i, 0)
    return pl.pallas_call(
        _combine_kernel,
        grid=(T // TB,),
        in_specs=[
            pl.BlockSpec((TB * TOP_K,), lambda i: (i,), memory_space=pltpu.SMEM),
            pl.BlockSpec(memory_space=pl.ANY),
            pl.BlockSpec((TB, D_MODEL), row),
            pl.BlockSpec((TB, TOP_K), row),
            pl.BlockSpec((1, D_MODEL), lambda i: (0, 0)),
        ],
        out_specs=pl.BlockSpec((TB, D_MODEL), row),
        out_shape=jax.ShapeDtypeStruct((T, D_MODEL), F32),
        scratch_shapes=[pltpu.VMEM((TB * TOP_K * ROW_TILES, LANES), F32),
                        pltpu.VMEM((TB, D_MODEL), F32), pltpu.SemaphoreType.DMA((TB // CT_COMB,))],
        compiler_params=_cparams(("arbitrary",)),
    )(dest_kmajor, ys, h1, wts, g_final)


def _rope_tables(positions):
    half = ROT_DIM // 2
    inv = ROPE_THETA ** (-jnp.arange(0, ROT_DIM, 2, dtype=F32) / ROT_DIM)
    ang = positions.astype(F32).reshape(-1, 1) * inv
    cos, sin = jnp.cos(ang), jnp.sin(ang)
    T = ang.shape[0]
    rest = HEAD_DIM - ROT_DIM
    c64 = jnp.concatenate([cos, cos, jnp.ones((T, rest), F32)], axis=-1)
    s1_64 = jnp.concatenate([-sin, jnp.zeros((T, HEAD_DIM - half), F32)], axis=-1)
    s2_64 = jnp.concatenate([jnp.zeros((T, half), F32), sin, jnp.zeros((T, rest), F32)], axis=-1)
    dup = lambda t: jnp.concatenate([t, t], axis=-1)
    return dup(c64), dup(s1_64), dup(s2_64)


def kernel(x, positions, norm_mix_g, w_in, gate_b, conv_dw_w, conv_dw_b, conv_ln_g, conv_ln_b, w_conv_out, lambda_q1, lambda_k1, lambda_q2, lambda_k2, attn_subln_g, w_attn_out, w_out, norm_ffn_g, router_w, router_b, w_gu, b_gu, w_down, b_down, norm_final_g):
    B, S, D = x.shape
    T = B * S
    x2 = x.reshape(T, D)
    ctab, s1tab, s2tab = _rope_tables(positions)

    proj = _inproj(x2, norm_mix_g[0].reshape(1, D), w_in[0].astype(BF16), ctab, s1tab, s2tab)
    m_a = _conv_branch(proj, B, S, conv_dw_w[0], conv_dw_b[0].reshape(1, -1),
                       conv_ln_g[0].reshape(1, -1), conv_ln_b[0].reshape(1, -1),
                       w_conv_out[0].astype(BF16), gate_b[0, :D].reshape(1, D))
    lam_vecs = jnp.stack([lambda_q1[0], lambda_k1[0], lambda_q2[0], lambda_k2[0]])
    attn_o = _attention(proj, B, S, lam_vecs, attn_subln_g[0].reshape(1, -1))
    h1, uf, logits_t = _mix(attn_o, proj, gate_b[0, D:].reshape(1, D), m_a, x2,
                            w_attn_out[0].astype(BF16), w_out[0].astype(BF16),
                            norm_ffn_g[0].reshape(1, D), router_w[0].T, router_b[0].reshape(-1, 1))
    logits = logits_t.T

    idx, wts, rank, counts = _route(logits)
    counts = counts[0].astype(jnp.int32)
    padded = (counts + M_BLK - 1) // M_BLK * M_BLK
    pad_end = jnp.cumsum(padded)
    pad_start = pad_end - padded
    dest = pad_start[idx] + rank
    n_blocks = (T * TOP_K) // M_BLK + N_EXPERTS
    n_rows = n_blocks * M_BLK
    blk_start = jnp.arange(n_blocks, dtype=jnp.int32) * M_BLK
    block_e = jnp.minimum(jnp.sum((pad_end[None, :] <= blk_start[:, None]).astype(jnp.int32), axis=1),
                          N_EXPERTS - 1)
    nvalid = (pad_end[-1:] // M_BLK).astype(jnp.int32)

    pad_lo = jnp.concatenate([pad_start + counts, nvalid]).astype(jnp.int32)
    pad_hi = jnp.concatenate([pad_end, jnp.full((1,), n_blocks)]).astype(jnp.int32)
    xs = _dispatch(pad_lo, pad_hi, dest.reshape(-1), uf, n_rows)
    bg = b_gu[0, :, 0::2].reshape(N_EXPERTS, 1, D_FF)
    bu = b_gu[0, :, 1::2].reshape(N_EXPERTS, 1, D_FF)
    blk_valid = jnp.arange(n_blocks, dtype=jnp.int32) < nvalid[0]
    later = jnp.logical_and(blk_valid[None, :], block_e[None, :] > block_e[:, None])
    nxt = jnp.min(jnp.where(later, block_e[None, :], N_EXPERTS), axis=1)
    nxt = jnp.where(nxt == N_EXPERTS, -1, nxt).astype(jnp.int32)
    act = _ffn_gu(block_e, nvalid, nxt, xs, w_gu[0], bg, bu)
    ys = _ffn_down(block_e, nvalid, nxt, act, w_down[0], b_down[0].reshape(N_EXPERTS, 1, D))

    dest_km = dest.reshape(T // CT_COMB, CT_COMB, TOP_K).transpose(0, 2, 1).reshape(-1)
    out = _combine(dest_km, ys, h1, wts, norm_final_g.reshape(1, D))
    return out.reshape(B, S, D)
```

```python
import functools

import jax
import jax.numpy as jnp
from jax import lax
from jax.experimental import pallas as pl
from jax.experimental.pallas import tpu as pltpu

F32 = jnp.float32
BF16 = jnp.bfloat16

D_MODEL = 2048
CHUNK = 64
CONV_DIM = 1024
CONV_WIDTH = 31
N_HEADS = 8
HEAD_DIM = 64
ATTN_DIM = N_HEADS * 2 * HEAD_DIM
ROT_DIM = HEAD_DIM // 4
ROPE_THETA = 500000.0
N_BRANCH = 2
IN_COLS = 2 * CONV_DIM + 3 * ATTN_DIM + N_BRANCH * D_MODEL
N_EXPERTS = 32
TOP_K = 4
D_FF = 2048
SWIGLU_LIMIT = 7.0
SWIGLU_ALPHA = 1.702
NORM_EPS = 1e-6
LN_EPS = 1e-5
LAM_INIT = 0.2

LANES = 128
SUBLANES = 8
VMEM_LIMIT = 56 * 1024 * 1024

COL_Q = 2 * CONV_DIM
COL_K = COL_Q + ATTN_DIM
COL_V = COL_K + ATTN_DIM
COL_G = COL_V + ATTN_DIM

TM_IN = 1024
TN_IN = 1024
TS_CONV = 256
HALO = 32
TQ = 512
TK = 512
RG = 128
KV_UNROLL = 4
assert KV_UNROLL == 4 and RG % CHUNK == 0 and TQ == TK
TM_MIX = 512
RG_MIX = 128
TR = 512
TB = 256
CT_COMB = 64
M_BLK = 512
TF = 1024
CAST_ROWS = 256
ROW_TILES = D_MODEL // LANES
PERM_W = 256


def _cparams(sem):
    return pltpu.CompilerParams(dimension_semantics=sem, vmem_limit_bytes=VMEM_LIMIT)


def _inproj_kernel(x_ref, g_ref, w_ref, c_ref, s1_ref, s2_ref, o_ref, u_ref):
    j = pl.program_id(1)

    @pl.when(j == 0)
    def _():
        x = x_ref[...]
        ms = jnp.mean(x * x, axis=-1, keepdims=True)
        u_ref[...] = (x * lax.rsqrt(ms + NORM_EPS) * g_ref[...]).astype(BF16)

    acc = jnp.dot(u_ref[...], w_ref[...], preferred_element_type=F32)
    q_tile = COL_Q // TN_IN
    k_tile = COL_K // TN_IN
    is_rope = jnp.logical_or(j == q_tile, j == k_tile)

    @pl.when(is_rope)
    def _():
        scale = jnp.where(j == q_tile, HEAD_DIM ** -0.5, 1.0).astype(F32)
        c = c_ref[...]
        s1 = s1_ref[...]
        s2 = s2_ref[...]
        for hh in range(TN_IN // LANES):
            t = acc[:, hh * LANES:(hh + 1) * LANES]
            r = (t * c + pltpu.roll(t, LANES - ROT_DIM // 2, 1) * s1
                 + pltpu.roll(t, ROT_DIM // 2, 1) * s2)
            o_ref[:, hh * LANES:(hh + 1) * LANES] = (r * scale).astype(BF16)

    @pl.when(jnp.logical_not(is_rope))
    def _():
        o_ref[...] = acc.astype(BF16)


def _inproj(x2, g, w_bf, ctab, s1tab, s2tab):
    T = x2.shape[0]
    return pl.pallas_call(
        _inproj_kernel,
        grid=(T // TM_IN, IN_COLS // TN_IN),
        in_specs=[
            pl.BlockSpec((TM_IN, D_MODEL), lambda i, j: (i, 0)),
            pl.BlockSpec((1, D_MODEL), lambda i, j: (0, 0)),
            pl.BlockSpec((D_MODEL, TN_IN), lambda i, j: (0, j)),
            pl.BlockSpec((TM_IN, LANES), lambda i, j: (i, 0)),
            pl.BlockSpec((TM_IN, LANES), lambda i, j: (i, 0)),
            pl.BlockSpec((TM_IN, LANES), lambda i, j: (i, 0)),
        ],
        out_specs=pl.BlockSpec((TM_IN, TN_IN), lambda i, j: (i, j)),
        out_shape=jax.ShapeDtypeStruct((T, IN_COLS), BF16),
        scratch_shapes=[pltpu.VMEM((TM_IN, D_MODEL), BF16)],
        compiler_params=_cparams(("parallel", "arbitrary")),
    )(x2, g, w_bf, ctab, s1tab, s2tab)


def _conv_kernel(val_ref, gate_ref, hval_ref, hgate_ref, dww_ref, dwb_ref, lng_ref, lnb_ref,
                 wco_ref, ga0_ref, ga1_ref, gab_ref, o_ref, hbuf, cbuf, shbuf):
    i = pl.program_id(1)
    ts = TS_CONV
    halo = hval_ref[...].astype(F32) * jax.nn.sigmoid(hgate_ref[...].astype(F32))
    hbuf[0:HALO, :] = jnp.where(i == 0, 0.0, halo)
    hbuf[HALO:HALO + ts, :] = val_ref[...].astype(F32) * jax.nn.sigmoid(gate_ref[...].astype(F32))

    base = HALO - (CONV_WIDTH - 1)

    def lane_chunk(c, carry):
        l0 = pl.multiple_of(c * LANES, LANES)
        acc = jnp.zeros((ts, LANES), F32) + dwb_ref[:, pl.ds(l0, LANES)]
        for r in range(SUBLANES):
            taps = [j for j in range(CONV_WIDTH) if (base + j) % SUBLANES == r]
            q_max = max((base + j) // SUBLANES for j in taps)
            n = SUBLANES * q_max + ts
            shbuf[0:n, :] = hbuf[r:r + n, pl.ds(l0, LANES)]
            for j in taps:
                q = (base + j) // SUBLANES
                acc = acc + dww_ref[j:j + 1, pl.ds(l0, LANES)] * shbuf[SUBLANES * q:SUBLANES * q + ts, :]
        cbuf[:, pl.ds(l0, LANES)] = acc
        return carry

    lax.fori_loop(0, CONV_DIM // LANES, lane_chunk, 0)

    y = cbuf[...]
    mu = jnp.mean(y, axis=-1, keepdims=True)
    yc = y - mu
    var = jnp.mean(yc * yc, axis=-1, keepdims=True)
    yn = yc * lax.rsqrt(var + LN_EPS) * lng_ref[...] + lnb_ref[...]
    a = yn * jax.nn.sigmoid(yn)
    out = jnp.dot(a.astype(BF16), wco_ref[...], preferred_element_type=F32)
    half = D_MODEL // 2
    g0 = jax.nn.sigmoid(ga0_ref[...].astype(F32) + gab_ref[:, 0:half])
    g1 = jax.nn.sigmoid(ga1_ref[...].astype(F32) + gab_ref[:, half:D_MODEL])
    o_ref[:, 0:half] = (out[:, 0:half] * g0).astype(BF16)
    o_ref[:, half:D_MODEL] = (out[:, half:D_MODEL] * g1).astype(BF16)


def _conv_branch(proj, B, S, dww, dwb, lng, lnb, wco_bf, gate_b_a):
    T = B * S
    nts = S // TS_CONV
    hpt = TS_CONV // HALO
    gcol = COL_G // CONV_DIM

    def halo_map(col):
        return lambda b, i: (jnp.maximum((b * nts + i) * hpt - 1, 0), col)

    const = lambda b, i: (0, 0)
    return pl.pallas_call(
        _conv_kernel,
        grid=(B, nts),
        in_specs=[
            pl.BlockSpec((TS_CONV, CONV_DIM), lambda b, i: (b * nts + i, 0)),
            pl.BlockSpec((TS_CONV, CONV_DIM), lambda b, i: (b * nts + i, 1)),
            pl.BlockSpec((HALO, CONV_DIM), halo_map(0)),
            pl.BlockSpec((HALO, CONV_DIM), halo_map(1)),
            pl.BlockSpec((CONV_WIDTH, CONV_DIM), const),
            pl.BlockSpec((1, CONV_DIM), const),
            pl.BlockSpec((1, CONV_DIM), const),
            pl.BlockSpec((1, CONV_DIM), const),
            pl.BlockSpec((CONV_DIM, D_MODEL), const),
            pl.BlockSpec((TS_CONV, CONV_DIM), lambda b, i: (b * nts + i, gcol)),
            pl.BlockSpec((TS_CONV, CONV_DIM), lambda b, i: (b * nts + i, gcol + 1)),
            pl.BlockSpec((1, D_MODEL), const),
        ],
        out_specs=pl.BlockSpec((TS_CONV, D_MODEL), lambda b, i: (b * nts + i, 0)),
        out_shape=jax.ShapeDtypeStruct((T, D_MODEL), BF16),
        scratch_shapes=[pltpu.VMEM((HALO + TS_CONV, CONV_DIM), F32),
                        pltpu.VMEM((TS_CONV, CONV_DIM), F32),
                        pltpu.VMEM((HALO + TS_CONV, LANES), F32)],
        compiler_params=_cparams(("parallel", "arbitrary")),
    )(proj, proj, proj, proj, dww, dwb, lng, lnb, wco_bf, proj, proj, gate_b_a)


def _attn_kernel(q_ref, k_ref, v_ref, lam_ref, sg_ref, o_ref, qs_ref, vx_ref, m_ref, acc_ref):
    qi = pl.program_id(2)
    hw = 2 * HEAD_DIM

    @pl.when(qi == 0)
    def _():
        vx_ref[:, 0:hw] = v_ref[...]
        vx_ref[:, hw:2 * hw] = jnp.ones((v_ref.shape[0], hw), BF16)

    q = q_ref[...]
    lane = lax.broadcasted_iota(jnp.int32, (TQ, hw), 1)
    zero = jnp.zeros_like(q)
    qs_ref[0:TQ, :] = jnp.where(lane < HEAD_DIM, q, zero)
    qs_ref[TQ:2 * TQ, :] = jnp.where(lane >= HEAD_DIM, q, zero)
    m_ref[...] = jnp.full(m_ref.shape, -jnp.inf, F32)
    acc_ref[...] = jnp.zeros(acc_ref.shape, F32)

    def step(kc, masked):
        k0 = pl.multiple_of(kc * TK, TK)
        kb = k_ref[pl.ds(k0, TK), :]
        vb = vx_ref[pl.ds(k0, TK), :]
        for g in range(2 * TQ // RG):
            r0 = g * RG
            s = lax.dot_general(qs_ref[r0:r0 + RG, :], kb, (((1,), (1,)), ((), ())),
                                preferred_element_type=F32)
            if masked:
                row = lax.broadcasted_iota(jnp.int32, (RG, TK), 0) + (r0 % TQ)
                col = lax.broadcasted_iota(jnp.int32, (RG, TK), 1)
                s = jnp.where(col // CHUNK <= row // CHUNK, s, -jnp.inf)
            m_old = m_ref[r0:r0 + RG, :]
            m_new = jnp.maximum(m_old, jnp.max(s, axis=-1, keepdims=True))
            alpha = jnp.exp(m_old - m_new)
            p = jnp.exp(s - jnp.tile(m_new, (1, TK // LANES)))
            pv = jnp.dot(p.astype(BF16), vb, preferred_element_type=F32)
            acc_ref[r0:r0 + RG, :] = jnp.tile(alpha, (1, 2)) * acc_ref[r0:r0 + RG, :] + pv
            m_ref[r0:r0 + RG, :] = m_new

    def body(j, carry):
        for u in range(KV_UNROLL):
            step(KV_UNROLL * j + u, False)
        return carry

    n_full = qi // KV_UNROLL
    lax.fori_loop(0, n_full, body, 0)
    rem = qi - n_full * KV_UNROLL
    for r in range(KV_UNROLL):
        @pl.when(rem == r)
        def _(r=r):
            for u in range(r):
                step(n_full * KV_UNROLL + u, False)
            step(qi, True)

    lam_v = lam_ref[...]
    lam = (jnp.exp(jnp.sum(lam_v[0:1, :] * lam_v[1:2, :], axis=-1, keepdims=True))
           - jnp.exp(jnp.sum(lam_v[2:3, :] * lam_v[3:4, :], axis=-1, keepdims=True)) + LAM_INIT)
    o0 = acc_ref[0:TQ, 0:hw] / acc_ref[0:TQ, hw:2 * hw]
    o1 = acc_ref[TQ:2 * TQ, 0:hw] / acc_ref[TQ:2 * TQ, hw:2 * hw]
    o = o0 - lam * o1
    ms = jnp.mean(o * o, axis=-1, keepdims=True)
    o = o * lax.rsqrt(ms + NORM_EPS) * sg_ref[...] * (1.0 - LAM_INIT)
    o_ref[...] = o.astype(BF16)


def _attention(proj, B, S, lam_vecs, subln_g):
    T = B * S
    nq = S // TQ
    hw = 2 * HEAD_DIM
    qc, kc, vc = COL_Q // hw, COL_K // hw, COL_V // hw
    return pl.pallas_call(
        _attn_kernel,
        grid=(B, N_HEADS, nq),
        in_specs=[
            pl.BlockSpec((TQ, hw), lambda b, h, i: (b * nq + i, qc + h)),
            pl.BlockSpec((S, hw), lambda b, h, i: (b, kc + h)),
            pl.BlockSpec((S, hw), lambda b, h, i: (b, vc + h)),
            pl.BlockSpec((4, HEAD_DIM), lambda b, h, i: (0, 0)),
            pl.BlockSpec((1, hw), lambda b, h, i: (0, 0)),
        ],
        out_specs=pl.BlockSpec((TQ, hw), lambda b, h, i: (b * nq + i, h)),
        out_shape=jax.ShapeDtypeStruct((T, ATTN_DIM), BF16),
        scratch_shapes=[pltpu.VMEM((2 * TQ, hw), BF16),
                        pltpu.VMEM((S, 2 * hw), BF16),
                        pltpu.VMEM((2 * TQ, LANES), F32),
                        pltpu.VMEM((2 * TQ, 2 * hw), F32)],
        compiler_params=_cparams(("parallel", "parallel", "arbitrary")),
    )(proj, proj, proj, lam_vecs, subln_g)


def _mix_kernel(a_ref, gb0_ref, gb1_ref, gbb_ref, ma_ref, x_ref, wao_ref, wo_ref, gf_ref,
                rwt_ref, rb_ref, h_ref, uf_ref, lg_ref):
    half = D_MODEL // 2
    for g in range(TM_MIX // RG_MIX):
        r0 = g * RG_MIX
        yb = jnp.dot(a_ref[r0:r0 + RG_MIX, :], wao_ref[...], preferred_element_type=F32)
        g0 = jax.nn.sigmoid(gb0_ref[r0:r0 + RG_MIX, :].astype(F32) + gbb_ref[:, 0:half])
        g1 = jax.nn.sigmoid(gb1_ref[r0:r0 + RG_MIX, :].astype(F32) + gbb_ref[:, half:D_MODEL])
        m0 = ma_ref[r0:r0 + RG_MIX, 0:half].astype(F32) + g0 * yb[:, 0:half]
        m1 = ma_ref[r0:r0 + RG_MIX, half:D_MODEL].astype(F32) + g1 * yb[:, half:D_MODEL]
        hh = (jnp.dot(m0.astype(BF16), wo_ref[0:half, :], preferred_element_type=F32)
              + jnp.dot(m1.astype(BF16), wo_ref[half:D_MODEL, :], preferred_element_type=F32))
        h = x_ref[r0:r0 + RG_MIX, :] + hh
        h_ref[r0:r0 + RG_MIX, :] = h
        ms = jnp.mean(h * h, axis=-1, keepdims=True)
        u = h * lax.rsqrt(ms + NORM_EPS) * gf_ref[...]
        for s in range(ROW_TILES):
            uf_ref[pl.ds(r0 * ROW_TILES + s, RG_MIX, stride=ROW_TILES), :] = u[:, s * LANES:(s + 1) * LANES]
        lg_ref[:, r0:r0 + RG_MIX] = lax.dot_general(
            rwt_ref[...], u, (((1,), (1,)), ((), ())), preferred_element_type=F32,
            precision=lax.Precision.HIGHEST) + rb_ref[...]


def _mix(attn_o, proj, gate_b_b, m_a, x2, wao_bf, wo_bf, g_ffn, router_w_t, router_b_col):
    T = x2.shape[0]
    gcol = (COL_G + D_MODEL) // CONV_DIM
    const = lambda i: (0, 0)
    row = lambda i: (i, 0)
    return pl.pallas_call(
        _mix_kernel,
        grid=(T // TM_MIX,),
        in_specs=[
            pl.BlockSpec((TM_MIX, ATTN_DIM), row),
            pl.BlockSpec((TM_MIX, CONV_DIM), lambda i: (i, gcol)),
            pl.BlockSpec((TM_MIX, CONV_DIM), lambda i: (i, gcol + 1)),
            pl.BlockSpec((1, D_MODEL), const),
            pl.BlockSpec((TM_MIX, D_MODEL), row),
            pl.BlockSpec((TM_MIX, D_MODEL), row),
            pl.BlockSpec((ATTN_DIM, D_MODEL), const, pipeline_mode=pl.Buffered(1)),
            pl.BlockSpec((D_MODEL, D_MODEL), const, pipeline_mode=pl.Buffered(1)),
            pl.BlockSpec((1, D_MODEL), const),
            pl.BlockSpec((N_EXPERTS, D_MODEL), const),
            pl.BlockSpec((N_EXPERTS, 1), const),
        ],
        out_specs=[
            pl.BlockSpec((TM_MIX, D_MODEL), row),
            pl.BlockSpec((TM_MIX * ROW_TILES, LANES), row),
            pl.BlockSpec((N_EXPERTS, TM_MIX), lambda i: (0, i)),
        ],
        out_shape=[
            jax.ShapeDtypeStruct((T, D_MODEL), F32),
            jax.ShapeDtypeStruct((T * ROW_TILES, LANES), F32),
            jax.ShapeDtypeStruct((N_EXPERTS, T), F32),
        ],
        compiler_params=_cparams(("parallel",)),
    )(attn_o, proj, proj, gate_b_b, m_a, x2, wao_bf, wo_bf, g_ffn, router_w_t, router_b_col)


def _route_kernel(lg_ref, idx_ref, w_ref, rank_ref, cnt_ref, carry_ref):
    i = pl.program_id(0)

    @pl.when(i == 0)
    def _():
        carry_ref[...] = jnp.zeros(carry_ref.shape, F32)

    vals = lg_ref[...]
    lane = lax.broadcasted_iota(jnp.int32, (TR, N_EXPERTS), 1)
    sel = jnp.zeros((TR, N_EXPERTS), F32)
    tops, idxs, hots = [], [], []
    for _ in range(TOP_K):
        m = jnp.max(vals, axis=-1, keepdims=True)
        ix = jnp.min(jnp.where(vals == m, lane, N_EXPERTS), axis=-1, keepdims=True)
        hot = lane == ix
        vals = jnp.where(hot, -jnp.inf, vals)
        sel = sel + hot.astype(F32)
        tops.append(m)
        idxs.append(ix)
        hots.append(hot)
    es = [jnp.exp(t - tops[0]) for t in tops]
    den = es[0] + es[1] + es[2] + es[3]

    r_i = lax.broadcasted_iota(jnp.int32, (TR, TR), 0)
    c_i = lax.broadcasted_iota(jnp.int32, (TR, TR), 1)
    tri = jnp.where(c_i < r_i, 1.0, 0.0).astype(BF16)
    before = jnp.dot(tri, sel.astype(BF16), preferred_element_type=F32) + carry_ref[...]
    carry_ref[...] = carry_ref[...] + jnp.sum(sel, axis=0, keepdims=True)
    cnt_ref[...] = carry_ref[...]

    lane4 = lax.broadcasted_iota(jnp.int32, (TR, TOP_K), 1)
    idx_o = jnp.zeros((TR, TOP_K), jnp.int32)
    w_o = jnp.zeros((TR, TOP_K), F32)
    rk_o = jnp.zeros((TR, TOP_K), F32)
    for k in range(TOP_K):
        rk = jnp.sum(jnp.where(hots[k], before, 0.0), axis=-1, keepdims=True)
        idx_o = jnp.where(lane4 == k, idxs[k], idx_o)
        w_o = jnp.where(lane4 == k, es[k] / den, w_o)
        rk_o = jnp.where(lane4 == k, rk, rk_o)
    idx_ref[...] = idx_o
    w_ref[...] = w_o
    rank_ref[...] = rk_o.astype(jnp.int32)


def _route(logits):
    T = logits.shape[0]
    row = lambda i: (i, 0)
    return pl.pallas_call(
        _route_kernel,
        grid=(T // TR,),
        in_specs=[pl.BlockSpec((TR, N_EXPERTS), row)],
        out_specs=[
            pl.BlockSpec((TR, TOP_K), row),
            pl.BlockSpec((TR, TOP_K), row),
            pl.BlockSpec((TR, TOP_K), row),
            pl.BlockSpec((1, N_EXPERTS), lambda i: (0, 0)),
        ],
        out_shape=[
            jax.ShapeDtypeStruct((T, TOP_K), jnp.int32),
            jax.ShapeDtypeStruct((T, TOP_K), F32),
            jax.ShapeDtypeStruct((T, TOP_K), jnp.int32),
            jax.ShapeDtypeStruct((1, N_EXPERTS), F32),
        ],
        scratch_shapes=[pltpu.VMEM((1, N_EXPERTS), F32)],
        compiler_params=_cparams(("arbitrary",)),
    )(logits)


def _dispatch_kernel(plo_ref, phi_ref, dest_ref, u_ref, xs_hbm, zbuf, sem, zsem):
    i = pl.program_id(0)

    @pl.when(i == 0)
    def _():
        zbuf[...] = jnp.zeros(zbuf.shape, F32)

    def copy(t, k):
        src = pl.multiple_of(t * ROW_TILES, ROW_TILES)
        dst = pl.multiple_of(dest_ref[t * TOP_K + k] * ROW_TILES, ROW_TILES)
        return pltpu.make_async_copy(u_ref.at[pl.ds(src, ROW_TILES)], xs_hbm.at[pl.ds(dst, ROW_TILES)], sem)

    def issue(t, carry):
        for k in range(TOP_K):
            copy(t, k).start()
        return carry

    def drain(t, carry):
        for k in range(TOP_K):
            copy(t, k).wait()
        return carry

    lax.fori_loop(0, TB, issue, 0, unroll=4)

    def zrow(r):
        dst = pl.multiple_of(r * ROW_TILES, ROW_TILES)
        return pltpu.make_async_copy(zbuf.at[pl.ds(0, ROW_TILES)], xs_hbm.at[pl.ds(dst, ROW_TILES)], zsem)

    def zblock(b):
        dst = pl.multiple_of(b * (M_BLK * ROW_TILES), M_BLK * ROW_TILES)
        return pltpu.make_async_copy(zbuf, xs_hbm.at[pl.ds(dst, M_BLK * ROW_TILES)], zsem)

    def span(fn):
        def go(r, carry):
            fn(r)
            return carry
        return go

    e = jnp.minimum(i, N_EXPERTS)
    lo = plo_ref[e]
    hi = phi_ref[e]
    n_row = jnp.where(i < N_EXPERTS, hi, lo)
    n_blk = jnp.where(i == N_EXPERTS, hi, lo)
    lax.fori_loop(lo, n_row, span(lambda r: zrow(r).start()), 0)
    lax.fori_loop(lo, n_blk, span(lambda b: zblock(b).start()), 0)
    lax.fori_loop(0, TB, drain, 0, unroll=4)
    lax.fori_loop(lo, n_row, span(lambda r: zrow(r).wait()), 0)
    lax.fori_loop(lo, n_blk, span(lambda b: zblock(b).wait()), 0)


def _dispatch(pad_lo, pad_hi, dest_flat, uf, n_rows):
    T = uf.shape[0] // ROW_TILES
    assert T // TB > N_EXPERTS
    grid_spec = pltpu.PrefetchScalarGridSpec(
        num_scalar_prefetch=2,
        grid=(T // TB,),
        in_specs=[
            pl.BlockSpec((TB * TOP_K,), lambda i, lo, hi: (i,), memory_space=pltpu.SMEM),
            pl.BlockSpec((TB * ROW_TILES, LANES), lambda i, lo, hi: (i, 0)),
        ],
        out_specs=pl.BlockSpec(memory_space=pl.ANY),
        scratch_shapes=[pltpu.VMEM((M_BLK * ROW_TILES, LANES), F32), pltpu.SemaphoreType.DMA,
                        pltpu.SemaphoreType.DMA],
    )
    return pl.pallas_call(
        _dispatch_kernel,
        grid_spec=grid_spec,
        out_shape=jax.ShapeDtypeStruct((n_rows * ROW_TILES, LANES), F32),
        compiler_params=_cparams(("arbitrary",)),
    )(pad_lo, pad_hi, dest_flat, uf)


def _ffn_gu_kernel(be_ref, nv_ref, nxt_ref, xs_ref, w_hbm, bg_ref, bu_ref, act_ref, stage, wg_s, wu_s, sem):
    n = pl.program_id(0)
    b = pl.program_id(1)
    valid = b < nv_ref[0]
    e = be_ref[b]
    new_w = jnp.logical_or(b == 0, e != be_ref[jnp.maximum(b - 1, 0)])

    def fetch(expert, tile):
        c0 = pl.multiple_of(tile * (2 * TF), 2 * TF)
        return pltpu.make_async_copy(w_hbm.at[expert, :, pl.ds(c0, 2 * TF)], stage, sem)

    @pl.when(jnp.logical_and(n == 0, b == 0))
    def _():
        fetch(e, n).start()

    @pl.when(jnp.logical_and(valid, new_w))
    def _():
        fetch(e, n).wait()
        r_i = lax.broadcasted_iota(jnp.int32, (PERM_W, PERM_W), 0)
        c_i = lax.broadcasted_iota(jnp.int32, (PERM_W, PERM_W), 1)
        src = jnp.where(c_i < PERM_W // 2, 2 * c_i, 2 * (c_i - PERM_W // 2) + 1)
        perm = jnp.where(r_i == src, 1.0, 0.0).astype(BF16)
        hw = PERM_W // 2
        for c in range(2 * TF // PERM_W):
            chunk = stage[:, c * PERM_W:(c + 1) * PERM_W].astype(BF16)
            res = jnp.dot(chunk, perm, preferred_element_type=F32).astype(BF16)
            wg_s[:, c * hw:(c + 1) * hw] = res[:, 0:hw]
            wu_s[:, c * hw:(c + 1) * hw] = res[:, hw:PERM_W]
        nx = nxt_ref[b]

        @pl.when(nx >= 0)
        def _():
            fetch(nx, n).start()

        @pl.when(jnp.logical_and(nx < 0, n + 1 < D_FF // TF))
        def _():
            fetch(be_ref[0], n + 1).start()

    @pl.when(valid)
    def _():
        x = jnp.concatenate(
            [xs_ref[pl.ds(s, M_BLK, stride=ROW_TILES), :].astype(BF16) for s in range(ROW_TILES)], axis=-1)
        gate = jnp.dot(x, wg_s[...], preferred_element_type=F32) + bg_ref[0]
        up = jnp.dot(x, wu_s[...], preferred_element_type=F32) + bu_ref[0]
        gate = jnp.minimum(gate, SWIGLU_LIMIT)
        up = jnp.clip(up, -SWIGLU_LIMIT, SWIGLU_LIMIT)
        act = (up + 1.0) * gate * jax.nn.sigmoid(SWIGLU_ALPHA * gate)
        act_ref[...] = act.astype(BF16)

    @pl.when(b >= nv_ref[0])
    def _():
        act_ref[...] = jnp.zeros(act_ref.shape, BF16)


def _ffn_down_kernel(be_ref, nv_ref, nxt_ref, act_ref, wd_hbm, bd_ref, y_ref, stage, wd_s, sem):
    b = pl.program_id(0)
    valid = b < nv_ref[0]
    e = be_ref[b]
    new_w = jnp.logical_or(b == 0, e != be_ref[jnp.maximum(b - 1, 0)])

    def fetch(expert):
        return pltpu.make_async_copy(wd_hbm.at[expert], stage, sem)

    @pl.when(b == 0)
    def _():
        fetch(e).start()

    @pl.when(jnp.logical_and(valid, new_w))
    def _():
        fetch(e).wait()
        for r in range(D_FF // CAST_ROWS):
            wd_s[r * CAST_ROWS:(r + 1) * CAST_ROWS, :] = stage[r * CAST_ROWS:(r + 1) * CAST_ROWS, :].astype(BF16)
        nx = nxt_ref[b]

        @pl.when(nx >= 0)
        def _():
            fetch(nx).start()

    @pl.when(valid)
    def _():
        y = jnp.dot(act_ref[...], wd_s[...], preferred_element_type=F32) + bd_ref[0]
        for s in range(ROW_TILES):
            y_ref[pl.ds(s, M_BLK, stride=ROW_TILES), :] = y[:, s * LANES:(s + 1) * LANES]

    @pl.when(b >= nv_ref[0])
    def _():
        y_ref[...] = jnp.zeros(y_ref.shape, F32)


def _blk(b, nv):
    return jnp.minimum(b, nv[0] - 1)


def _ffn_gu(block_e, nvalid, nxt, xs, w_gu, bg, bu):
    n_rows = xs.shape[0] // ROW_TILES
    nb = n_rows // M_BLK
    grid_spec = pltpu.PrefetchScalarGridSpec(
        num_scalar_prefetch=3,
        grid=(D_FF // TF, nb),
        in_specs=[
            pl.BlockSpec((M_BLK * ROW_TILES, LANES), lambda n, b, be, nv, nx: (_blk(b, nv), 0)),
            pl.BlockSpec(memory_space=pl.ANY),
            pl.BlockSpec((1, 1, TF), lambda n, b, be, nv, nx: (be[_blk(b, nv)], 0, n)),
            pl.BlockSpec((1, 1, TF), lambda n, b, be, nv, nx: (be[_blk(b, nv)], 0, n)),
        ],
        out_specs=pl.BlockSpec((M_BLK, TF), lambda n, b, be, nv, nx: (b, n)),
        scratch_shapes=[pltpu.VMEM((D_MODEL, 2 * TF), F32), pltpu.VMEM((D_MODEL, TF), BF16),
                        pltpu.VMEM((D_MODEL, TF), BF16), pltpu.SemaphoreType.DMA],
    )
    return pl.pallas_call(
        _ffn_gu_kernel,
        grid_spec=grid_spec,
        out_shape=jax.ShapeDtypeStruct((n_rows, D_FF), BF16),
        compiler_params=_cparams(("arbitrary", "arbitrary")),
    )(block_e, nvalid, nxt, xs, w_gu, bg, bu)


def _ffn_down(block_e, nvalid, nxt, act, wd, bd):
    n_rows = act.shape[0]
    nb = n_rows // M_BLK
    grid_spec = pltpu.PrefetchScalarGridSpec(
        num_scalar_prefetch=3,
        grid=(nb,),
        in_specs=[
            pl.BlockSpec((M_BLK, D_FF), lambda b, be, nv, nx: (_blk(b, nv), 0)),
            pl.BlockSpec(memory_space=pl.ANY),
            pl.BlockSpec((1, 1, D_MODEL), lambda b, be, nv, nx: (be[_blk(b, nv)], 0, 0)),
        ],
        out_specs=pl.BlockSpec((M_BLK * ROW_TILES, LANES), lambda b, be, nv, nx: (b, 0)),
        scratch_shapes=[pltpu.VMEM((D_FF, D_MODEL), F32), pltpu.VMEM((D_FF, D_MODEL), BF16),
                        pltpu.SemaphoreType.DMA],
    )
    return pl.pallas_call(
        _ffn_down_kernel,
        grid_spec=grid_spec,
        out_shape=jax.ShapeDtypeStruct((n_rows * ROW_TILES, LANES), F32),
        compiler_params=_cparams(("arbitrary",)),
    )(block_e, nvalid, nxt, act, wd, bd)


def _combine_kernel(dest_ref, ys_hbm, h_ref, w_ref, gfin_ref, o_ref, buf, hbuf, sems):
    rows_c = TOP_K * CT_COMB

    def copy(c, j):
        r = c * rows_c + j
        src = pl.multiple_of(dest_ref[r] * ROW_TILES, ROW_TILES)
        dst = pl.multiple_of(r * ROW_TILES, ROW_TILES)
        return pltpu.make_async_copy(ys_hbm.at[pl.ds(src, ROW_TILES)], buf.at[pl.ds(dst, ROW_TILES)],
                                     sems.at[c])

    for c in range(TB // CT_COMB):
        def issue(j, carry, c=c):
            copy(c, j).start()
            return carry
        lax.fori_loop(0, rows_c, issue, 0, unroll=8)

    for c in range(TB // CT_COMB):
        def drain(j, carry, c=c):
            copy(c, j).wait()
            return carry
        lax.fori_loop(0, rows_c, drain, 0, unroll=8)

        t0 = c * CT_COMB
        w = w_ref[t0:t0 + CT_COMB, :]
        for s in range(ROW_TILES):
            hs = h_ref[t0:t0 + CT_COMB, s * LANES:(s + 1) * LANES]
            for k in range(TOP_K):
                first = (c * rows_c + k * CT_COMB) * ROW_TILES + s
                hs = hs + w[:, k:k + 1] * buf[pl.ds(first, CT_COMB, stride=ROW_TILES), :]
            hbuf[t0:t0 + CT_COMB, s * LANES:(s + 1) * LANES] = hs
        h = hbuf[t0:t0 + CT_COMB, :]
        ms = jnp.mean(h * h, axis=-1, keepdims=True)
        o_ref[t0:t0 + CT_COMB, :] = h * lax.rsqrt(ms + NORM_EPS) * gfin_ref[...]


def _combine(dest_kmajor, ys, h1, wts, g_final):
    T = h1.shape[0]
    row = lambda i: (i, 0)
    return pl.pallas_call(
        _combine_kernel,
        grid=(T // TB,),
        in_specs=[
            pl.BlockSpec((TB * TOP_K,), lambda i: (i,), memory_space=pltpu.SMEM),
            pl.BlockSpec(memory_space=pl.ANY),
            pl.BlockSpec((TB, D_MODEL), row),
            pl.BlockSpec((TB, TOP_K), row),
            pl.BlockSpec((1, D_MODEL), lambda i: (0, 0)),
        ],
        out_specs=pl.BlockSpec((TB, D_MODEL), row),
        out_shape=jax.ShapeDtypeStruct((T, D_MODEL), F32),
        scratch_shapes=[pltpu.VMEM((TB * TOP_K * ROW_TILES, LANES), F32),
                        pltpu.VMEM((TB, D_MODEL), F32), pltpu.SemaphoreType.DMA((TB // CT_COMB,))],
        compiler_params=_cparams(("arbitrary",)),
    )(dest_kmajor, ys, h1, wts, g_final)


def _rope_tables(positions):
    half = ROT_DIM // 2
    inv = ROPE_THETA ** (-jnp.arange(0, ROT_DIM, 2, dtype=F32) / ROT_DIM)
    ang = positions.astype(F32).reshape(-1, 1) * inv
    cos, sin = jnp.cos(ang), jnp.sin(ang)
    T = ang.shape[0]
    rest = HEAD_DIM - ROT_DIM
    c64 = jnp.concatenate([cos, cos, jnp.ones((T, rest), F32)], axis=-1)
    s1_64 = jnp.concatenate([-sin, jnp.zeros((T, HEAD_DIM - half), F32)], axis=-1)
    s2_64 = jnp.concatenate([jnp.zeros((T, half), F32), sin, jnp.zeros((T, rest), F32)], axis=-1)
    dup = lambda t: jnp.concatenate([t, t], axis=-1)
    return dup(c64), dup(s1_64), dup(s2_64)


def kernel(x, positions, norm_mix_g, w_in, gate_b, conv_dw_w, conv_dw_b, conv_ln_g, conv_ln_b, w_conv_out, lambda_q1, lambda_k1, lambda_q2, lambda_k2, attn_subln_g, w_attn_out, w_out, norm_ffn_g, router_w, router_b, w_gu, b_gu, w_down, b_down, norm_final_g):
    B, S, D = x.shape
    T = B * S
    x2 = x.reshape(T, D)
    ctab, s1tab, s2tab = _rope_tables(positions)

    proj = _inproj(x2, norm_mix_g[0].reshape(1, D), w_in[0].astype(BF16), ctab, s1tab, s2tab)
    m_a = _conv_branch(proj, B, S, conv_dw_w[0], conv_dw_b[0].reshape(1, -1),
                       conv_ln_g[0].reshape(1, -1), conv_ln_b[0].reshape(1, -1),
                       w_conv_out[0].astype(BF16), gate_b[0, :D].reshape(1, D))
    lam_vecs = jnp.stack([lambda_q1[0], lambda_k1[0], lambda_q2[0], lambda_k2[0]])
    attn_o = _attention(proj, B, S, lam_vecs, attn_subln_g[0].reshape(1, -1))
    h1, uf, logits_t = _mix(attn_o, proj, gate_b[0, D:].reshape(1, D), m_a, x2,
                            w_attn_out[0].astype(BF16), w_out[0].astype(BF16),
                            norm_ffn_g[0].reshape(1, D), router_w[0].T, router_b[0].reshape(-1, 1))
    logits = logits_t.T

    idx, wts, rank, counts = _route(logits)
    counts = counts[0].astype(jnp.int32)
    padded = (counts + M_BLK - 1) // M_BLK * M_BLK
    pad_end = jnp.cumsum(padded)
    pad_start = pad_end - padded
    dest = pad_start[idx] + rank
    n_blocks = (T * TOP_K) // M_BLK + N_EXPERTS
    n_rows = n_blocks * M_BLK
    blk_start = jnp.arange(n_blocks, dtype=jnp.int32) * M_BLK
    block_e = jnp.minimum(jnp.sum((pad_end[None, :] <= blk_start[:, None]).astype(jnp.int32), axis=1),
                          N_EXPERTS - 1)
    nvalid = (pad_end[-1:] // M_BLK).astype(jnp.int32)

    pad_lo = jnp.concatenate([pad_start + counts, nvalid]).astype(jnp.int32)
    pad_hi = jnp.concatenate([pad_end, jnp.full((1,), n_blocks)]).astype(jnp.int32)
    xs = _dispatch(pad_lo, pad_hi, dest.reshape(-1), uf, n_rows)
    bg = b_gu[0, :, 0::2].reshape(N_EXPERTS, 1, D_FF)
    bu = b_gu[0, :, 1::2].reshape(N_EXPERTS, 1, D_FF)
    blk_valid = jnp.arange(n_blocks, dtype=jnp.int32) < nvalid[0]
    later = jnp.logical_and(blk_valid[None, :], block_e[None, :] > block_e[:, None])
    nxt = jnp.min(jnp.where(later, block_e[None, :], N_EXPERTS), axis=1)
    nxt = jnp.where(nxt == N_EXPERTS, -1, nxt).astype(jnp.int32)
    act = _ffn_gu(block_e, nvalid, nxt, xs, w_gu[0], bg, bu)
    ys = _ffn_down(block_e, nvalid, nxt, act, w_down[0], b_down[0].reshape(N_EXPERTS, 1, D))

    dest_km = dest.reshape(T // CT_COMB, CT_COMB, TOP_K).transpose(0, 2, 1).reshape(-1)
    out = _combine(dest_km, ys, h1, wts, norm_final_g.reshape(1, D))
    return out.reshape(B, S, D)
```

```python
import functools

import jax
import jax.numpy as jnp
from jax import lax
from jax.experimental import pallas as pl
from jax.experimental.pallas import tpu as pltpu

F32 = jnp.float32
BF16 = jnp.bfloat16

D_MODEL = 2048
CHUNK = 64
CONV_DIM = 1024
CONV_WIDTH = 31
N_HEADS = 8
HEAD_DIM = 64
ATTN_DIM = N_HEADS * 2 * HEAD_DIM
ROT_DIM = HEAD_DIM // 4
ROPE_THETA = 500000.0
N_BRANCH = 2
IN_COLS = 2 * CONV_DIM + 3 * ATTN_DIM + N_BRANCH * D_MODEL
N_EXPERTS = 32
TOP_K = 4
D_FF = 2048
SWIGLU_LIMIT = 7.0
SWIGLU_ALPHA = 1.702
NORM_EPS = 1e-6
LN_EPS = 1e-5
LAM_INIT = 0.2

LANES = 128
SUBLANES = 8
VMEM_LIMIT = 56 * 1024 * 1024

COL_Q = 2 * CONV_DIM
COL_K = COL_Q + ATTN_DIM
COL_V = COL_K + ATTN_DIM
COL_G = COL_V + ATTN_DIM

TM_IN = 1024
TN_IN = 1024
TS_CONV = 256
HALO = 32
TQ = 512
TK = 512
RG = 128
KV_UNROLL = 4
assert KV_UNROLL == 4 and RG % CHUNK == 0 and TQ == TK
TM_MIX = 512
RG_MIX = 128
TR = 512
TB = 256
CT_COMB = 64
M_BLK = 512
TF = 1024
CAST_ROWS = 256
ROW_TILES = D_MODEL // LANES
PERM_W = 256


def _cparams(sem):
    return pltpu.CompilerParams(dimension_semantics=sem, vmem_limit_bytes=VMEM_LIMIT)


def _inproj_kernel(x_ref, g_ref, w_ref, c_ref, s1_ref, s2_ref, o_ref, u_ref):
    j = pl.program_id(1)

    @pl.when(j == 0)
    def _():
        x = x_ref[...]
        ms = jnp.mean(x * x, axis=-1, keepdims=True)
        u_ref[...] = (x * lax.rsqrt(ms + NORM_EPS) * g_ref[...]).astype(BF16)

    acc = jnp.dot(u_ref[...], w_ref[...], preferred_element_type=F32)
    q_tile = COL_Q // TN_IN
    k_tile = COL_K // TN_IN
    is_rope = jnp.logical_or(j == q_tile, j == k_tile)

    @pl.when(is_rope)
    def _():
        scale = jnp.where(j == q_tile, HEAD_DIM ** -0.5, 1.0).astype(F32)
        c = c_ref[...]
        s1 = s1_ref[...]
        s2 = s2_ref[...]
        for hh in range(TN_IN // LANES):
            t = acc[:, hh * LANES:(hh + 1) * LANES]
            r = (t * c + pltpu.roll(t, LANES - ROT_DIM // 2, 1) * s1
                 + pltpu.roll(t, ROT_DIM // 2, 1) * s2)
            o_ref[:, hh * LANES:(hh + 1) * LANES] = (r * scale).astype(BF16)

    @pl.when(jnp.logical_not(is_rope))
    def _():
        o_ref[...] = acc.astype(BF16)


def _inproj(x2, g, w_bf, ctab, s1tab, s2tab):
    T = x2.shape[0]
    return pl.pallas_call(
        _inproj_kernel,
        grid=(T // TM_IN, IN_COLS // TN_IN),
        in_specs=[
            pl.BlockSpec((TM_IN, D_MODEL), lambda i, j: (i, 0)),
            pl.BlockSpec((1, D_MODEL), lambda i, j: (0, 0)),
            pl.BlockSpec((D_MODEL, TN_IN), lambda i, j: (0, j)),
            pl.BlockSpec((TM_IN, LANES), lambda i, j: (i, 0)),
            pl.BlockSpec((TM_IN, LANES), lambda i, j: (i, 0)),
            pl.BlockSpec((TM_IN, LANES), lambda i, j: (i, 0)),
        ],
        out_specs=pl.BlockSpec((TM_IN, TN_IN), lambda i, j: (i, j)),
        out_shape=jax.ShapeDtypeStruct((T, IN_COLS), BF16),
        scratch_shapes=[pltpu.VMEM((TM_IN, D_MODEL), BF16)],
        compiler_params=_cparams(("parallel", "arbitrary")),
    )(x2, g, w_bf, ctab, s1tab, s2tab)


def _conv_kernel(val_ref, gate_ref, hval_ref, hgate_ref, dww_ref, dwb_ref, lng_ref, lnb_ref,
                 wco_ref, ga0_ref, ga1_ref, gab_ref, o_ref, hbuf, cbuf, shbuf):
    i = pl.program_id(1)
    ts = TS_CONV
    halo = hval_ref[...].astype(F32) * jax.nn.sigmoid(hgate_ref[...].astype(F32))
    hbuf[0:HALO, :] = jnp.where(i == 0, 0.0, halo)
    hbuf[HALO:HALO + ts, :] = val_ref[...].astype(F32) * jax.nn.sigmoid(gate_ref[...].astype(F32))

    base = HALO - (CONV_WIDTH - 1)

    def lane_chunk(c, carry):
        l0 = pl.multiple_of(c * LANES, LANES)
        acc = jnp.zeros((ts, LANES), F32) + dwb_ref[:, pl.ds(l0, LANES)]
        for r in range(SUBLANES):
            taps = [j for j in range(CONV_WIDTH) if (base + j) % SUBLANES == r]
            q_max = max((base + j) // SUBLANES for j in taps)
            n = SUBLANES * q_max + ts
            shbuf[0:n, :] = hbuf[r:r + n, pl.ds(l0, LANES)]
            for j in taps:
                q = (base + j) // SUBLANES
                acc = acc + dww_ref[j:j + 1, pl.ds(l0, LANES)] * shbuf[SUBLANES * q:SUBLANES * q + ts, :]
        cbuf[:, pl.ds(l0, LANES)] = acc
        return carry

    lax.fori_loop(0, CONV_DIM // LANES, lane_chunk, 0)

    y = cbuf[...]
    mu = jnp.mean(y, axis=-1, keepdims=True)
    yc = y - mu
    var = jnp.mean(yc * yc, axis=-1, keepdims=True)
    yn = yc * lax.rsqrt(var + LN_EPS) * lng_ref[...] + lnb_ref[...]
    a = yn * jax.nn.sigmoid(yn)
    out = jnp.dot(a.astype(BF16), wco_ref[...], preferred_element_type=F32)
    half = D_MODEL // 2
    g0 = jax.nn.sigmoid(ga0_ref[...].astype(F32) + gab_ref[:, 0:half])
    g1 = jax.nn.sigmoid(ga1_ref[...].astype(F32) + gab_ref[:, half:D_MODEL])
    o_ref[:, 0:half] = (out[:, 0:half] * g0).astype(BF16)
    o_ref[:, half:D_MODEL] = (out[:, half:D_MODEL] * g1).astype(BF16)


def _conv_branch(proj, B, S, dww, dwb, lng, lnb, wco_bf, gate_b_a):
    T = B * S
    nts = S // TS_CONV
    hpt = TS_CONV // HALO
    gcol = COL_G // CONV_DIM

    def halo_map(col):
        return lambda b, i: (jnp.maximum((b * nts + i) * hpt - 1, 0), col)

    const = lambda b, i: (0, 0)
    return pl.pallas_call(
        _conv_kernel,
        grid=(B, nts),
        in_specs=[
            pl.BlockSpec((TS_CONV, CONV_DIM), lambda b, i: (b * nts + i, 0)),
            pl.BlockSpec((TS_CONV, CONV_DIM), lambda b, i: (b * nts + i, 1)),
            pl.BlockSpec((HALO, CONV_DIM), halo_map(0)),
            pl.BlockSpec((HALO, CONV_DIM), halo_map(1)),
            pl.BlockSpec((CONV_WIDTH, CONV_DIM), const),
            pl.BlockSpec((1, CONV_DIM), const),
            pl.BlockSpec((1, CONV_DIM), const),
            pl.BlockSpec((1, CONV_DIM), const),
            pl.BlockSpec((CONV_DIM, D_MODEL), const),
            pl.BlockSpec((TS_CONV, CONV_DIM), lambda b, i: (b * nts + i, gcol)),
            pl.BlockSpec((TS_CONV, CONV_DIM), lambda b, i: (b * nts + i, gcol + 1)),
            pl.BlockSpec((1, D_MODEL), const),
        ],
        out_specs=pl.BlockSpec((TS_CONV, D_MODEL), lambda b, i: (b * nts + i, 0)),
        out_shape=jax.ShapeDtypeStruct((T, D_MODEL), BF16),
        scratch_shapes=[pltpu.VMEM((HALO + TS_CONV, CONV_DIM), F32),
                        pltpu.VMEM((TS_CONV, CONV_DIM), F32),
                        pltpu.VMEM((HALO + TS_CONV, LANES), F32)],
        compiler_params=_cparams(("parallel", "arbitrary")),
    )(proj, proj, proj, proj, dww, dwb, lng, lnb, wco_bf, proj, proj, gate_b_a)


def _attn_kernel(q_ref, k_ref, v_ref, lam_ref, sg_ref, o_ref, qs_ref, vx_ref, m_ref, acc_ref):
    qi = pl.program_id(2)
    hw = 2 * HEAD_DIM

    @pl.when(qi == 0)
    def _():
        vx_ref[:, 0:hw] = v_ref[...]
        vx_ref[:, hw:2 * hw] = jnp.ones((v_ref.shape[0], hw), BF16)

    q = q_ref[...]
    lane = lax.broadcasted_iota(jnp.int32, (TQ, hw), 1)
    zero = jnp.zeros_like(q)
    qs_ref[0:TQ, :] = jnp.where(lane < HEAD_DIM, q, zero)
    qs_ref[TQ:2 * TQ, :] = jnp.where(lane >= HEAD_DIM, q, zero)
    m_ref[...] = jnp.full(m_ref.shape, -jnp.inf, F32)
    acc_ref[...] = jnp.zeros(acc_ref.shape, F32)

    def step(kc, masked):
        k0 = pl.multiple_of(kc * TK, TK)
        kb = k_ref[pl.ds(k0, TK), :]
        vb = vx_ref[pl.ds(k0, TK), :]
        for g in range(2 * TQ // RG):
            r0 = g * RG
            s = lax.dot_general(qs_ref[r0:r0 + RG, :], kb, (((1,), (1,)), ((), ())),
                                preferred_element_type=F32)
            if masked:
                row = lax.broadcasted_iota(jnp.int32, (RG, TK), 0) + (r0 % TQ)
                col = lax.broadcasted_iota(jnp.int32, (RG, TK), 1)
                s = jnp.where(col // CHUNK <= row // CHUNK, s, -jnp.inf)
            m_old = m_ref[r0:r0 + RG, :]
            m_new = jnp.maximum(m_old, jnp.max(s, axis=-1, keepdims=True))
            alpha = jnp.exp(m_old - m_new)
            p = jnp.exp(s - jnp.tile(m_new, (1, TK // LANES)))
            pv = jnp.dot(p.astype(BF16), vb, preferred_element_type=F32)
            acc_ref[r0:r0 + RG, :] = jnp.tile(alpha, (1, 2)) * acc_ref[r0:r0 + RG, :] + pv
            m_ref[r0:r0 + RG, :] = m_new

    def body(j, carry):
        for u in range(KV_UNROLL):
            step(KV_UNROLL * j + u, False)
        return carry

    n_full = qi // KV_UNROLL
    lax.fori_loop(0, n_full, body, 0)
    rem = qi - n_full * KV_UNROLL
    for r in range(KV_UNROLL):
        @pl.when(rem == r)
        def _(r=r):
            for u in range(r):
                step(n_full * KV_UNROLL + u, False)
            step(qi, True)

    lam_v = lam_ref[...]
    lam = (jnp.exp(jnp.sum(lam_v[0:1, :] * lam_v[1:2, :], axis=-1, keepdims=True))
           - jnp.exp(jnp.sum(lam_v[2:3, :] * lam_v[3:4, :], axis=-1, keepdims=True)) + LAM_INIT)
    o0 = acc_ref[0:TQ, 0:hw] / acc_ref[0:TQ, hw:2 * hw]
    o1 = acc_ref[TQ:2 * TQ, 0:hw] / acc_ref[TQ:2 * TQ, hw:2 * hw]
    o = o0 - lam * o1
    ms = jnp.mean(o * o, axis=-1, keepdims=True)
    o = o * lax.rsqrt(ms + NORM_EPS) * sg_ref[...] * (1.0 - LAM_INIT)
    o_ref[...] = o.astype(BF16)


def _attention(proj, B, S, lam_vecs, subln_g):
    T = B * S
    nq = S // TQ
    hw = 2 * HEAD_DIM
    qc, kc, vc = COL_Q // hw, COL_K // hw, COL_V // hw
    return pl.pallas_call(
        _attn_kernel,
        grid=(B, N_HEADS, nq),
        in_specs=[
            pl.BlockSpec((TQ, hw), lambda b, h, i: (b * nq + i, qc + h)),
            pl.BlockSpec((S, hw), lambda b, h, i: (b, kc + h)),
            pl.BlockSpec((S, hw), lambda b, h, i: (b, vc + h)),
            pl.BlockSpec((4, HEAD_DIM), lambda b, h, i: (0, 0)),
            pl.BlockSpec((1, hw), lambda b, h, i: (0, 0)),
        ],
        out_specs=pl.BlockSpec((TQ, hw), lambda b, h, i: (b * nq + i, h)),
        out_shape=jax.ShapeDtypeStruct((T, ATTN_DIM), BF16),
        scratch_shapes=[pltpu.VMEM((2 * TQ, hw), BF16),
                        pltpu.VMEM((S, 2 * hw), BF16),
                        pltpu.VMEM((2 * TQ, LANES), F32),
                        pltpu.VMEM((2 * TQ, 2 * hw), F32)],
        compiler_params=_cparams(("parallel", "parallel", "arbitrary")),
    )(proj, proj, proj, lam_vecs, subln_g)


def _mix_kernel(a_ref, gb0_ref, gb1_ref, gbb_ref, ma_ref, x_ref, wao_ref, wo_ref, gf_ref,
                rwt_ref, rb_ref, h_ref, uf_ref, lg_ref):
    half = D_MODEL // 2
    for g in range(TM_MIX // RG_MIX):
        r0 = g * RG_MIX
        yb = jnp.dot(a_ref[r0:r0 + RG_MIX, :], wao_ref[...], preferred_element_type=F32)
        g0 = jax.nn.sigmoid(gb0_ref[r0:r0 + RG_MIX, :].astype(F32) + gbb_ref[:, 0:half])
        g1 = jax.nn.sigmoid(gb1_ref[r0:r0 + RG_MIX, :].astype(F32) + gbb_ref[:, half:D_MODEL])
        m0 = ma_ref[r0:r0 + RG_MIX, 0:half].astype(F32) + g0 * yb[:, 0:half]
        m1 = ma_ref[r0:r0 + RG_MIX, half:D_MODEL].astype(F32) + g1 * yb[:, half:D_MODEL]
        hh = (jnp.dot(m0.astype(BF16), wo_ref[0:half, :], preferred_element_type=F32)
              + jnp.dot(m1.astype(BF16), wo_ref[half:D_MODEL, :], preferred_element_type=F32))
        h = x_ref[r0:r0 + RG_MIX, :] + hh
        h_ref[r0:r0 + RG_MIX, :] = h
        ms = jnp.mean(h * h, axis=-1, keepdims=True)
        u = h * lax.rsqrt(ms + NORM_EPS) * gf_ref[...]
        for s in range(ROW_TILES):
            uf_ref[pl.ds(r0 * ROW_TILES + s, RG_MIX, stride=ROW_TILES), :] = u[:, s * LANES:(s + 1) * LANES]
        lg_ref[:, r0:r0 + RG_MIX] = lax.dot_general(
            rwt_ref[...], u, (((1,), (1,)), ((), ())), preferred_element_type=F32,
            precision=lax.Precision.HIGHEST) + rb_ref[...]


def _mix(attn_o, proj, gate_b_b, m_a, x2, wao_bf, wo_bf, g_ffn, router_w_t, router_b_col):
    T = x2.shape[0]
    gcol = (COL_G + D_MODEL) // CONV_DIM
    const = lambda i: (0, 0)
    row = lambda i: (i, 0)
    return pl.pallas_call(
        _mix_kernel,
        grid=(T // TM_MIX,),
        in_specs=[
            pl.BlockSpec((TM_MIX, ATTN_DIM), row),
            pl.BlockSpec((TM_MIX, CONV_DIM), lambda i: (i, gcol)),
            pl.BlockSpec((TM_MIX, CONV_DIM), lambda i: (i, gcol + 1)),
            pl.BlockSpec((1, D_MODEL), const),
            pl.BlockSpec((TM_MIX, D_MODEL), row),
            pl.BlockSpec((TM_MIX, D_MODEL), row),
            pl.BlockSpec((ATTN_DIM, D_MODEL), const, pipeline_mode=pl.Buffered(1)),
            pl.BlockSpec((D_MODEL, D_MODEL), const, pipeline_mode=pl.Buffered(1)),
            pl.BlockSpec((1, D_MODEL), const),
            pl.BlockSpec((N_EXPERTS, D_MODEL), const),
            pl.BlockSpec((N_EXPERTS, 1), const),
        ],
        out_specs=[
            pl.BlockSpec((TM_MIX, D_MODEL), row),
            pl.BlockSpec((TM_MIX * ROW_TILES, LANES), row),
            pl.BlockSpec((N_EXPERTS, TM_MIX), lambda i: (0, i)),
        ],
        out_shape=[
            jax.ShapeDtypeStruct((T, D_MODEL), F32),
            jax.ShapeDtypeStruct((T * ROW_TILES, LANES), F32),
            jax.ShapeDtypeStruct((N_EXPERTS, T), F32),
        ],
        compiler_params=_cparams(("parallel",)),
    )(attn_o, proj, proj, gate_b_b, m_a, x2, wao_bf, wo_bf, g_ffn, router_w_t, router_b_col)


def _route_kernel(lg_ref, idx_ref, w_ref, rank_ref, cnt_ref, carry_ref):
    i = pl.program_id(0)

    @pl.when(i == 0)
    def _():
        carry_ref[...] = jnp.zeros(carry_ref.shape, F32)

    vals = lg_ref[...]
    lane = lax.broadcasted_iota(jnp.int32, (TR, N_EXPERTS), 1)
    sel = jnp.zeros((TR, N_EXPERTS), F32)
    tops, idxs, hots = [], [], []
    for _ in range(TOP_K):
        m = jnp.max(vals, axis=-1, keepdims=True)
        ix = jnp.min(jnp.where(vals == m, lane, N_EXPERTS), axis=-1, keepdims=True)
        hot = lane == ix
        vals = jnp.where(hot, -jnp.inf, vals)
        sel = sel + hot.astype(F32)
        tops.append(m)
        idxs.append(ix)
        hots.append(hot)
    es = [jnp.exp(t - tops[0]) for t in tops]
    den = es[0] + es[1] + es[2] + es[3]

    r_i = lax.broadcasted_iota(jnp.int32, (TR, TR), 0)
    c_i = lax.broadcasted_iota(jnp.int32, (TR, TR), 1)
    tri = jnp.where(c_i < r_i, 1.0, 0.0).astype(BF16)
    before = jnp.dot(tri, sel.astype(BF16), preferred_element_type=F32) + carry_ref[...]
    carry_ref[...] = carry_ref[...] + jnp.sum(sel, axis=0, keepdims=True)
    cnt_ref[...] = carry_ref[...]

    lane4 = lax.broadcasted_iota(jnp.int32, (TR, TOP_K), 1)
    idx_o = jnp.zeros((TR, TOP_K), jnp.int32)
    w_o = jnp.zeros((TR, TOP_K), F32)
    rk_o = jnp.zeros((TR, TOP_K), F32)
    for k in range(TOP_K):
        rk = jnp.sum(jnp.where(hots[k], before, 0.0), axis=-1, keepdims=True)
        idx_o = jnp.where(lane4 == k, idxs[k], idx_o)
        w_o = jnp.where(lane4 == k, es[k] / den, w_o)
        rk_o = jnp.where(lane4 == k, rk, rk_o)
    idx_ref[...] = idx_o
    w_ref[...] = w_o
    rank_ref[...] = rk_o.astype(jnp.int32)


def _route(logits):
    T = logits.shape[0]
    row = lambda i: (i, 0)
    return pl.pallas_call(
        _route_kernel,
        grid=(T // TR,),
        in_specs=[pl.BlockSpec((TR, N_EXPERTS), row)],
        out_specs=[
            pl.BlockSpec((TR, TOP_K), row),
            pl.BlockSpec((TR, TOP_K), row),
            pl.BlockSpec((TR, TOP_K), row),
            pl.BlockSpec((1, N_EXPERTS), lambda i: (0, 0)),
        ],
        out_shape=[
            jax.ShapeDtypeStruct((T, TOP_K), jnp.int32),
            jax.ShapeDtypeStruct((T, TOP_K), F32),
            jax.ShapeDtypeStruct((T, TOP_K), jnp.int32),
            jax.ShapeDtypeStruct((1, N_EXPERTS), F32),
        ],
        scratch_shapes=[pltpu.VMEM((1, N_EXPERTS), F32)],
        compiler_params=_cparams(("arbitrary",)),
    )(logits)


def _dispatch_kernel(plo_ref, phi_ref, dest_ref, u_ref, xs_hbm, zbuf, sem, zsem):
    i = pl.program_id(0)

    @pl.when(i == 0)
    def _():
        zbuf[...] = jnp.zeros(zbuf.shape, F32)

    def copy(t, k):
        src = pl.multiple_of(t * ROW_TILES, ROW_TILES)
        dst = pl.multiple_of(dest_ref[t * TOP_K + k] * ROW_TILES, ROW_TILES)
        return pltpu.make_async_copy(u_ref.at[pl.ds(src, ROW_TILES)], xs_hbm.at[pl.ds(dst, ROW_TILES)], sem)

    def issue(t, carry):
        for k in range(TOP_K):
            copy(t, k).start(priority=k % 2)
        return carry

    def drain(t, carry):
        for k in range(TOP_K):
            copy(t, k).wait()
        return carry

    lax.fori_loop(0, TB, issue, 0, unroll=4)

    def zrow(r):
        dst = pl.multiple_of(r * ROW_TILES, ROW_TILES)
        return pltpu.make_async_copy(zbuf.at[pl.ds(0, ROW_TILES)], xs_hbm.at[pl.ds(dst, ROW_TILES)], zsem)

    def zblock(b):
        dst = pl.multiple_of(b * (M_BLK * ROW_TILES), M_BLK * ROW_TILES)
        return pltpu.make_async_copy(zbuf, xs_hbm.at[pl.ds(dst, M_BLK * ROW_TILES)], zsem)

    def span(fn):
        def go(r, carry):
            fn(r)
            return carry
        return go

    e = jnp.minimum(i, N_EXPERTS)
    lo = plo_ref[e]
    hi = phi_ref[e]
    n_row = jnp.where(i < N_EXPERTS, hi, lo)
    n_blk = jnp.where(i == N_EXPERTS, hi, lo)
    lax.fori_loop(lo, n_row, span(lambda r: zrow(r).start()), 0)
    lax.fori_loop(lo, n_blk, span(lambda b: zblock(b).start()), 0)
    lax.fori_loop(0, TB, drain, 0, unroll=4)
    lax.fori_loop(lo, n_row, span(lambda r: zrow(r).wait()), 0)
    lax.fori_loop(lo, n_blk, span(lambda b: zblock(b).wait()), 0)


def _dispatch(pad_lo, pad_hi, dest_flat, uf, n_rows):
    T = uf.shape[0] // ROW_TILES
    assert T // TB > N_EXPERTS
    grid_spec = pltpu.PrefetchScalarGridSpec(
        num_scalar_prefetch=2,
        grid=(T // TB,),
        in_specs=[
            pl.BlockSpec((TB * TOP_K,), lambda i, lo, hi: (i,), memory_space=pltpu.SMEM),
            pl.BlockSpec((TB * ROW_TILES, LANES), lambda i, lo, hi: (i, 0)),
        ],
        out_specs=pl.BlockSpec(memory_space=pl.ANY),
        scratch_shapes=[pltpu.VMEM((M_BLK * ROW_TILES, LANES), F32), pltpu.SemaphoreType.DMA,
                        pltpu.SemaphoreType.DMA],
    )
    return pl.pallas_call(
        _dispatch_kernel,
        grid_spec=grid_spec,
        out_shape=jax.ShapeDtypeStruct((n_rows * ROW_TILES, LANES), F32),
        compiler_params=_cparams(("arbitrary",)),
    )(pad_lo, pad_hi, dest_flat, uf)


def _ffn_gu_kernel(be_ref, nv_ref, nxt_ref, xs_ref, w_hbm, bg_ref, bu_ref, act_ref, stage, wg_s, wu_s, sem):
    n = pl.program_id(0)
    b = pl.program_id(1)
    valid = b < nv_ref[0]
    e = be_ref[b]
    new_w = jnp.logical_or(b == 0, e != be_ref[jnp.maximum(b - 1, 0)])

    def fetch(expert, tile):
        c0 = pl.multiple_of(tile * (2 * TF), 2 * TF)
        return pltpu.make_async_copy(w_hbm.at[expert, :, pl.ds(c0, 2 * TF)], stage, sem)

    @pl.when(jnp.logical_and(n == 0, b == 0))
    def _():
        fetch(e, n).start()

    @pl.when(jnp.logical_and(valid, new_w))
    def _():
        fetch(e, n).wait()
        r_i = lax.broadcasted_iota(jnp.int32, (PERM_W, PERM_W), 0)
        c_i = lax.broadcasted_iota(jnp.int32, (PERM_W, PERM_W), 1)
        src = jnp.where(c_i < PERM_W // 2, 2 * c_i, 2 * (c_i - PERM_W // 2) + 1)
        perm = jnp.where(r_i == src, 1.0, 0.0).astype(BF16)
        hw = PERM_W // 2
        for c in range(2 * TF // PERM_W):
            chunk = stage[:, c * PERM_W:(c + 1) * PERM_W].astype(BF16)
            res = jnp.dot(chunk, perm, preferred_element_type=F32).astype(BF16)
            wg_s[:, c * hw:(c + 1) * hw] = res[:, 0:hw]
            wu_s[:, c * hw:(c + 1) * hw] = res[:, hw:PERM_W]
        nx = nxt_ref[b]

        @pl.when(nx >= 0)
        def _():
            fetch(nx, n).start()

        @pl.when(jnp.logical_and(nx < 0, n + 1 < D_FF // TF))
        def _():
            fetch(be_ref[0], n + 1).start()

    @pl.when(valid)
    def _():
        x = jnp.concatenate(
            [xs_ref[pl.ds(s, M_BLK, stride=ROW_TILES), :].astype(BF16) for s in range(ROW_TILES)], axis=-1)
        gate = jnp.dot(x, wg_s[...], preferred_element_type=F32) + bg_ref[0]
        up = jnp.dot(x, wu_s[...], preferred_element_type=F32) + bu_ref[0]
        gate = jnp.minimum(gate, SWIGLU_LIMIT)
        up = jnp.clip(up, -SWIGLU_LIMIT, SWIGLU_LIMIT)
        act = (up + 1.0) * gate * jax.nn.sigmoid(SWIGLU_ALPHA * gate)
        act_ref[...] = act.astype(BF16)

    @pl.when(b >= nv_ref[0])
    def _():
        act_ref[...] = jnp.zeros(act_ref.shape, BF16)


def _ffn_down_kernel(be_ref, nv_ref, nxt_ref, act_ref, wd_hbm, bd_ref, y_ref, stage, wd_s, sem):
    b = pl.program_id(0)
    valid = b < nv_ref[0]
    e = be_ref[b]
    new_w = jnp.logical_or(b == 0, e != be_ref[jnp.maximum(b - 1, 0)])

    def fetch(expert):
        return pltpu.make_async_copy(wd_hbm.at[expert], stage, sem)

    @pl.when(b == 0)
    def _():
        fetch(e).start()

    @pl.when(jnp.logical_and(valid, new_w))
    def _():
        fetch(e).wait()
        for r in range(D_FF // CAST_ROWS):
            wd_s[r * CAST_ROWS:(r + 1) * CAST_ROWS, :] = stage[r * CAST_ROWS:(r + 1) * CAST_ROWS, :].astype(BF16)
        nx = nxt_ref[b]

        @pl.when(nx >= 0)
        def _():
            fetch(nx).start()

    @pl.when(valid)
    def _():
        y = jnp.dot(act_ref[...], wd_s[...], preferred_element_type=F32) + bd_ref[0]
        for s in range(ROW_TILES):
            y_ref[pl.ds(s, M_BLK, stride=ROW_TILES), :] = y[:, s * LANES:(s + 1) * LANES]

    @pl.when(b >= nv_ref[0])
    def _():
        y_ref[...] = jnp.zeros(y_ref.shape, F32)


def _blk(b, nv):
    return jnp.minimum(b, nv[0] - 1)


def _ffn_gu(block_e, nvalid, nxt, xs, w_gu, bg, bu):
    n_rows = xs.shape[0] // ROW_TILES
    nb = n_rows // M_BLK
    grid_spec = pltpu.PrefetchScalarGridSpec(
        num_scalar_prefetch=3,
        grid=(D_FF // TF, nb),
        in_specs=[
            pl.BlockSpec((M_BLK * ROW_TILES, LANES), lambda n, b, be, nv, nx: (_blk(b, nv), 0)),
            pl.BlockSpec(memory_space=pl.ANY),
            pl.BlockSpec((1, 1, TF), lambda n, b, be, nv, nx: (be[_blk(b, nv)], 0, n)),
            pl.BlockSpec((1, 1, TF), lambda n, b, be, nv, nx: (be[_blk(b, nv)], 0, n)),
        ],
        out_specs=pl.BlockSpec((M_BLK, TF), lambda n, b, be, nv, nx: (b, n)),
        scratch_shapes=[pltpu.VMEM((D_MODEL, 2 * TF), F32), pltpu.VMEM((D_MODEL, TF), BF16),
                        pltpu.VMEM((D_MODEL, TF), BF16), pltpu.SemaphoreType.DMA],
    )
    return pl.pallas_call(
        _ffn_gu_kernel,
        grid_spec=grid_spec,
        out_shape=jax.ShapeDtypeStruct((n_rows, D_FF), BF16),
        compiler_params=_cparams(("arbitrary", "arbitrary")),
    )(block_e, nvalid, nxt, xs, w_gu, bg, bu)


def _ffn_down(block_e, nvalid, nxt, act, wd, bd):
    n_rows = act.shape[0]
    nb = n_rows // M_BLK
    grid_spec = pltpu.PrefetchScalarGridSpec(
        num_scalar_prefetch=3,
        grid=(nb,),
        in_specs=[
            pl.BlockSpec((M_BLK, D_FF), lambda b, be, nv, nx: (_blk(b, nv), 0)),
            pl.BlockSpec(memory_space=pl.ANY),
            pl.BlockSpec((1, 1, D_MODEL), lambda b, be, nv, nx: (be[_blk(b, nv)], 0, 0)),
        ],
        out_specs=pl.BlockSpec((M_BLK * ROW_TILES, LANES), lambda b, be, nv, nx: (b, 0)),
        scratch_shapes=[pltpu.VMEM((D_FF, D_MODEL), F32), pltpu.VMEM((D_FF, D_MODEL), BF16),
                        pltpu.SemaphoreType.DMA],
    )
    return pl.pallas_call(
        _ffn_down_kernel,
        grid_spec=grid_spec,
        out_shape=jax.ShapeDtypeStruct((n_rows * ROW_TILES, LANES), F32),
        compiler_params=_cparams(("arbitrary",)),
    )(block_e, nvalid, nxt, act, wd, bd)


def _combine_kernel(dest_ref, ys_hbm, h_ref, w_ref, gfin_ref, o_ref, buf, hbuf, sems):
    rows_c = TOP_K * CT_COMB

    def copy(c, j):
        r = c * rows_c + j
        src = pl.multiple_of(dest_ref[r] * ROW_TILES, ROW_TILES)
        dst = pl.multiple_of(r * ROW_TILES, ROW_TILES)
        return pltpu.make_async_copy(ys_hbm.at[pl.ds(src, ROW_TILES)], buf.at[pl.ds(dst, ROW_TILES)],
                                     sems.at[c])

    for c in range(TB // CT_COMB):
        def issue(j, carry, c=c):
            copy(c, 2 * j).start(priority=0)
            copy(c, 2 * j + 1).start(priority=1)
            return carry
        lax.fori_loop(0, rows_c // 2, issue, 0, unroll=4)

    for c in range(TB // CT_COMB):
        def drain(j, carry, c=c):
            copy(c, j).wait()
            return carry
        lax.fori_loop(0, rows_c, drain, 0, unroll=8)

        t0 = c * CT_COMB
        w = w_ref[t0:t0 + CT_COMB, :]
        for s in range(ROW_TILES):
            hs = h_ref[t0:t0 + CT_COMB, s * LANES:(s + 1) * LANES]
            for k in range(TOP_K):
                first = (c * rows_c + k * CT_COMB) * ROW_TILES + s
                hs = hs + w[:, k:k + 1] * buf[pl.ds(first, CT_COMB, stride=ROW_TILES), :]
            hbuf[t0:t0 + CT_COMB, s * LANES:(s + 1) * LANES] = hs
        h = hbuf[t0:t0 + CT_COMB, :]
        ms = jnp.mean(h * h, axis=-1, keepdims=True)
        o_ref[t0:t0 + CT_COMB, :] = h * lax.rsqrt(ms + NORM_EPS) * gfin_ref[...]


def _combine(dest_kmajor, ys, h1, wts, g_final):
    T = h1.shape[0]
    row = lambda i: (i, 0)
    return pl.pallas_call(
        _combine_kernel,
        grid=(T // TB,),
        in_specs=[
            pl.BlockSpec((TB * TOP_K,), lambda i: (i,), memory_space=pltpu.SMEM),
            pl.BlockSpec(memory_space=pl.ANY),
            pl.BlockSpec((TB, D_MODEL), row),
            pl.BlockSpec((TB, TOP_K), row),
            pl.BlockSpec((1, D_MODEL), lambda i: (0, 0)),
        ],
        out_specs=pl.BlockSpec((TB, D_MODEL), row),
        out_shape=jax.ShapeDtypeStruct((T, D_MODEL), F32),
        scratch_shapes=[pltpu.VMEM((TB * TOP_K * ROW_TILES, LANES), F32),
                        pltpu.VMEM((TB, D_MODEL), F32), pltpu.SemaphoreType.DMA((TB // CT_COMB,))],
        compiler_params=_cparams(("arbitrary",)),
    )(dest_kmajor, ys, h1, wts, g_final)


def _rope_tables(positions):
    half = ROT_DIM // 2
    inv = ROPE_THETA ** (-jnp.arange(0, ROT_DIM, 2, dtype=F32) / ROT_DIM)
    ang = positions.astype(F32).reshape(-1, 1) * inv
    cos, sin = jnp.cos(ang), jnp.sin(ang)
    T = ang.shape[0]
    rest = HEAD_DIM - ROT_DIM
    c64 = jnp.concatenate([cos, cos, jnp.ones((T, rest), F32)], axis=-1)
    s1_64 = jnp.concatenate([-sin, jnp.zeros((T, HEAD_DIM - half), F32)], axis=-1)
    s2_64 = jnp.concatenate([jnp.zeros((T, half), F32), sin, jnp.zeros((T, rest), F32)], axis=-1)
    dup = lambda t: jnp.concatenate([t, t], axis=-1)
    return dup(c64), dup(s1_64), dup(s2_64)


def kernel(x, positions, norm_mix_g, w_in, gate_b, conv_dw_w, conv_dw_b, conv_ln_g, conv_ln_b, w_conv_out, lambda_q1, lambda_k1, lambda_q2, lambda_k2, attn_subln_g, w_attn_out, w_out, norm_ffn_g, router_w, router_b, w_gu, b_gu, w_down, b_down, norm_final_g):
    B, S, D = x.shape
    T = B * S
    x2 = x.reshape(T, D)
    ctab, s1tab, s2tab = _rope_tables(positions)

    proj = _inproj(x2, norm_mix_g[0].reshape(1, D), w_in[0].astype(BF16), ctab, s1tab, s2tab)
    m_a = _conv_branch(proj, B, S, conv_dw_w[0], conv_dw_b[0].reshape(1, -1),
                       conv_ln_g[0].reshape(1, -1), conv_ln_b[0].reshape(1, -1),
                       w_conv_out[0].astype(BF16), gate_b[0, :D].reshape(1, D))
    lam_vecs = jnp.stack([lambda_q1[0], lambda_k1[0], lambda_q2[0], lambda_k2[0]])
    attn_o = _attention(proj, B, S, lam_vecs, attn_subln_g[0].reshape(1, -1))
    h1, uf, logits_t = _mix(attn_o, proj, gate_b[0, D:].reshape(1, D), m_a, x2,
                            w_attn_out[0].astype(BF16), w_out[0].astype(BF16),
                            norm_ffn_g[0].reshape(1, D), router_w[0].T, router_b[0].reshape(-1, 1))
    logits = logits_t.T

    idx, wts, rank, counts = _route(logits)
    counts = counts[0].astype(jnp.int32)
    padded = (counts + M_BLK - 1) // M_BLK * M_BLK
    pad_end = jnp.cumsum(padded)
    pad_start = pad_end - padded
    dest = pad_start[idx] + rank
    n_blocks = (T * TOP_K) // M_BLK + N_EXPERTS
    n_rows = n_blocks * M_BLK
    blk_start = jnp.arange(n_blocks, dtype=jnp.int32) * M_BLK
    block_e = jnp.minimum(jnp.sum((pad_end[None, :] <= blk_start[:, None]).astype(jnp.int32), axis=1),
                          N_EXPERTS - 1)
    nvalid = (pad_end[-1:] // M_BLK).astype(jnp.int32)

    pad_lo = jnp.concatenate([pad_start + counts, nvalid]).astype(jnp.int32)
    pad_hi = jnp.concatenate([pad_end, jnp.full((1,), n_blocks)]).astype(jnp.int32)
    xs = _dispatch(pad_lo, pad_hi, dest.reshape(-1), uf, n_rows)
    bg = b_gu[0, :, 0::2].reshape(N_EXPERTS, 1, D_FF)
    bu = b_gu[0, :, 1::2].reshape(N_EXPERTS, 1, D_FF)
    blk_valid = jnp.arange(n_blocks, dtype=jnp.int32) < nvalid[0]
    later = jnp.logical_and(blk_valid[None, :], block_e[None, :] > block_e[:, None])
    nxt = jnp.min(jnp.where(later, block_e[None, :], N_EXPERTS), axis=1)
    nxt = jnp.where(nxt == N_EXPERTS, -1, nxt).astype(jnp.int32)
    act = _ffn_gu(block_e, nvalid, nxt, xs, w_gu[0], bg, bu)
    ys = _ffn_down(block_e, nvalid, nxt, act, w_down[0], b_down[0].reshape(N_EXPERTS, 1, D))

    dest_km = dest.reshape(T // CT_COMB, CT_COMB, TOP_K).transpose(0, 2, 1).reshape(-1)
    out = _combine(dest_km, ys, h1, wts, norm_final_g.reshape(1, D))
    return out.reshape(B, S, D)
```
